```python
import jax, jax.numpy as jnp
from jax import lax
import numpy as np

D_MODEL = 1024
BATCH = 4
SEQ = 4096
DEPTH = 1
DEC_BATCH = 128
DEC_SEQ = 8
PAST_LEN = 16384
PAGE_SIZE = 128

HEAD_DIM = 64
SWA_KV_HEADS = 2
SWA_GROUP = 3
SWA_WINDOW = 128
DIL_HEADS = 6
DIL_PATTERNS = ((128, 1), (512, 4), (2048, 16))
DIL_MAX_WINDOW = 2048
MEM_HEADS = 4
MEM_TOKENS = 256
BLK = 128
ROPE_THETA = 10000.0
EPS = 1e-6
SCALE = HEAD_DIM ** -0.5
Q_A = SWA_KV_HEADS * SWA_GROUP * HEAD_DIM
KV_A = SWA_KV_HEADS * HEAD_DIM
Q_B = DIL_HEADS * HEAD_DIM
KV_B = DIL_HEADS * HEAD_DIM
Q_X = MEM_HEADS * HEAD_DIM
IN_SIZES = (Q_A, KV_A, KV_A, Q_B, KV_B, KV_B, Q_X)
IN_WIDTH = Q_A + 2 * KV_A + Q_B + 2 * KV_B + Q_X
MIX_WIDTH = Q_A + Q_B + Q_X
D_FF = ((8 * D_MODEL + 3 * 256 - 1) // (3 * 256)) * 256

kernel_name = "hymba_swa_sink_dilated_memxattn_decoder_step"


def rms_norm(x, g):
    xf = x.astype(jnp.float32)
    y = xf * lax.rsqrt(jnp.mean(xf * xf, axis=-1, keepdims=True) + EPS)
    return (y * g.astype(jnp.float32)).astype(x.dtype)


def rotary(x, pos):
    half = HEAD_DIM // 2
    inv = ROPE_THETA ** (-jnp.arange(half, dtype=jnp.float32) / half)
    ang = pos.astype(jnp.float32)[:, None] * inv[None, :]
    shape = (1, pos.shape[0]) + (1,) * (x.ndim - 3) + (half,)
    cos = jnp.cos(ang).reshape(shape)
    sin = jnp.sin(ang).reshape(shape)
    xf = x.astype(jnp.float32)
    x1, x2 = xf[..., :half], xf[..., half:]
    return jnp.concatenate([x1 * cos - x2 * sin, x2 * cos + x1 * sin], axis=-1).astype(x.dtype)


def banded_stats(q, k, v, window):
    b, t, hk, g, dh = q.shape
    nb = -(-t // BLK)
    pad = nb * BLK - t
    qb = jnp.pad(q, ((0, 0), (0, pad), (0, 0), (0, 0), (0, 0))).reshape(b, nb, BLK, hk, g, dh)
    kp = jnp.pad(k, ((0, 0), (BLK, pad), (0, 0), (0, 0))).reshape(b, nb + 1, BLK, hk, dh)
    vp = jnp.pad(v, ((0, 0), (BLK, pad), (0, 0), (0, 0))).reshape(b, nb + 1, BLK, hk, dh)
    kw = jnp.concatenate([kp[:, :-1], kp[:, 1:]], axis=2)
    vw = jnp.concatenate([vp[:, :-1], vp[:, 1:]], axis=2)
    s = jnp.einsum('bnqhgd,bnkhd->bnhgqk', qb, kw, preferred_element_type=jnp.float32) * SCALE
    qi = jnp.arange(BLK)[:, None]
    kj = jnp.arange(2 * BLK)[None, :]
    dist = qi + BLK - kj
    kabs = jnp.arange(nb)[:, None, None] * BLK + kj[None] - BLK
    mask = (dist >= 0)[None] & (dist <= window)[None] & (kabs >= 0)
    s = jnp.where(mask[None, :, None, None], s, -jnp.inf)
    m = jnp.max(s, axis=-1, keepdims=True)
    p = jnp.exp(s - m)
    l = jnp.sum(p, axis=-1)
    o = jnp.einsum('bnhgqk,bnkhd->bnqhgd', p, vw.astype(jnp.float32))
    o = o.reshape(b, nb * BLK, hk, g, dh)[:, :t]
    m = jnp.moveaxis(m[..., 0], -1, 2).reshape(b, nb * BLK, hk, g)[:, :t]
    l = jnp.moveaxis(l, -1, 2).reshape(b, nb * BLK, hk, g)[:, :t]
    return o, m, l


def window_gather_stats(q, k_all, v_all, start, window, stride):
    s_len = q.shape[1]
    n_k = window // stride + 1
    idx = start + jnp.arange(s_len)[:, None] - stride * jnp.arange(n_k)[None, :]
    valid = idx >= 0
    idx = jnp.maximum(idx, 0)
    kg = k_all[:, idx]
    vg = v_all[:, idx]
    sc = jnp.einsum('nshgd,nskhd->nshgk', q, kg, preferred_element_type=jnp.float32) * SCALE
    sc = jnp.where(valid[None, :, None, None, :], sc, -jnp.inf)
    m = jnp.max(sc, axis=-1, keepdims=True)
    p = jnp.exp(sc - m)
    l = jnp.sum(p, axis=-1)
    o = jnp.einsum('nshgk,nskhd->nshgd', p, vg.astype(jnp.float32))
    return o, m[..., 0], l


def sink_combine(o, m, l, sink):
    sink = sink.astype(jnp.float32)
    mm = jnp.maximum(m, sink)
    a = jnp.exp(m - mm)
    den = l * a + jnp.exp(sink - mm)
    return o * (a / den)[..., None]


def dilation_combine(stats):
    m_top = jnp.max(jnp.stack([m for _, m, _ in stats]), axis=0)
    num = sum(o * jnp.exp(m - m_top)[..., None] for o, m, _ in stats)
    den = sum(l * jnp.exp(m - m_top) for _, m, l in stats)
    return num / den[..., None]


def fold_stride(x, r):
    b, t = x.shape[:2]
    x = jnp.moveaxis(x.reshape((b, t // r, r) + x.shape[2:]), 2, 1)
    return x.reshape((b * r, t // r) + x.shape[3:])


def unfold_stride(x, b, r):
    br, tr = x.shape[:2]
    x = jnp.moveaxis(x.reshape((b, r, tr) + x.shape[2:]), 1, 2)
    return x.reshape((b, tr * r) + x.shape[3:])


def dilated_prompt(q, k, v):
    b = q.shape[0]
    stats = []
    for w, r in DIL_PATTERNS:
        o, m, l = banded_stats(fold_stride(q, r), fold_stride(k, r), fold_stride(v, r), w // r)
        stats.append((unfold_stride(o, b, r), unfold_stride(m, b, r), unfold_stride(l, b, r)))
    return dilation_combine(stats)


def dilated_sample(q, k_all, v_all, start):
    stats = [window_gather_stats(q, k_all, v_all, start, w, r) for w, r in DIL_PATTERNS]
    return dilation_combine(stats)


def mix_inputs(u, w_in, pos):
    b, t = u.shape[:2]
    z = jnp.einsum('btd,de->bte', u, w_in)
    qa, ka, va, qb, kb, vb, qx = jnp.split(z, np.cumsum(IN_SIZES)[:-1], axis=-1)
    qa = rotary(qa.reshape(b, t, SWA_KV_HEADS, SWA_GROUP, HEAD_DIM), pos)
    ka = rotary(ka.reshape(b, t, SWA_KV_HEADS, HEAD_DIM), pos)
    va = va.reshape(b, t, SWA_KV_HEADS, HEAD_DIM)
    qb = rotary(qb.reshape(b, t, DIL_HEADS, 1, HEAD_DIM), pos)
    kb = rotary(kb.reshape(b, t, DIL_HEADS, HEAD_DIM), pos)
    vb = vb.reshape(b, t, DIL_HEADS, HEAD_DIM)
    qx = qx.reshape(b, t, MEM_HEADS, HEAD_DIM)
    return qa, ka, va, qb, kb, vb, qx


def mem_kv(mem, w_mem_kv):
    b, n = mem.shape[:2]
    mk, mv = jnp.split(jnp.einsum('bmd,de->bme', mem, w_mem_kv), 2, axis=-1)
    return mk.reshape(b, n, MEM_HEADS, HEAD_DIM), mv.reshape(b, n, MEM_HEADS, HEAD_DIM)


def mem_attend(qx, mk, mv):
    s = jnp.einsum('bthd,bmhd->bhtm', qx, mk, preferred_element_type=jnp.float32) * SCALE
    p = jax.nn.softmax(s, axis=-1)
    return jnp.einsum('bhtm,bmhd->bthd', p, mv.astype(jnp.float32))


def swiglu(h, w_gate, w_up, w_down):
    g = jnp.einsum('btd,df->btf', h, w_gate)
    u = jnp.einsum('btd,df->btf', h, w_up)
    return jnp.einsum('btf,fd->btd', jax.nn.silu(g) * u, w_down)


def finish_layer(x, oa, ob, ox, w_o, g_post_mix, g_pre_ffn, w_gate, w_up, w_down, g_post_ffn):
    b, t = x.shape[:2]
    cat = jnp.concatenate([oa.reshape(b, t, -1), ob.reshape(b, t, -1), ox.reshape(b, t, -1)],
                          axis=-1).astype(x.dtype)
    x = x + rms_norm(jnp.einsum('bte,ed->btd', cat, w_o), g_post_mix)
    x = x + rms_norm(swiglu(rms_norm(x, g_pre_ffn), w_gate, w_up, w_down), g_post_ffn)
    return x


def setup_inputs(seed: int = 0) -> dict:
    key = jax.random.key(seed)
    ks = jax.random.split(key, 24)
    f32 = jnp.float32
    lb_a = min(SWA_WINDOW, PAST_LEN)
    lb_b = min(DIL_MAX_WINDOW, PAST_LEN)

    def nrm(k, shape, scale=1.0):
        return scale * jax.random.normal(k, shape, f32)

    def gain(k):
        return 1.0 + 0.02 * jax.random.normal(k, (DEPTH, D_MODEL), f32)

    return {
        "x_prompt": nrm(ks[0], (BATCH, SEQ, D_MODEL)),
        "x_sample": nrm(ks[1], (DEC_BATCH, DEC_SEQ, D_MODEL)),
        "cache_swa_k": nrm(ks[2], (DEPTH, DEC_BATCH, lb_a, SWA_KV_HEADS, HEAD_DIM)),
        "cache_swa_v": nrm(ks[3], (DEPTH, DEC_BATCH, lb_a, SWA_KV_HEADS, HEAD_DIM)),
        "cache_dil_k": nrm(ks[4], (DEPTH, DEC_BATCH, lb_b, DIL_HEADS, HEAD_DIM)),
        "cache_dil_v": nrm(ks[5], (DEPTH, DEC_BATCH, lb_b, DIL_HEADS, HEAD_DIM)),
        "cache_mem_k": nrm(ks[6], (DEPTH, DEC_BATCH, MEM_TOKENS, MEM_HEADS, HEAD_DIM)),
        "cache_mem_v": nrm(ks[7], (DEPTH, DEC_BATCH, MEM_TOKENS, MEM_HEADS, HEAD_DIM)),
        "mem_prompt": nrm(ks[8], (BATCH, MEM_TOKENS, D_MODEL)),
        "g_pre_mix": gain(ks[9]),
        "w_in": nrm(ks[10], (DEPTH, D_MODEL, IN_WIDTH), D_MODEL ** -0.5),
        "sinks": nrm(ks[11], (DEPTH, SWA_KV_HEADS, SWA_GROUP)),
        "w_mem_kv": nrm(ks[12], (DEPTH, D_MODEL, 2 * MEM_HEADS * HEAD_DIM), D_MODEL ** -0.5),
        "w_o": nrm(ks[13], (DEPTH, MIX_WIDTH, D_MODEL), MIX_WIDTH ** -0.5),
        "g_post_mix": gain(ks[14]),
        "g_pre_ffn": gain(ks[15]),
        "w_gate": nrm(ks[16], (DEPTH, D_MODEL, D_FF), D_MODEL ** -0.5),
        "w_up": nrm(ks[17], (DEPTH, D_MODEL, D_FF), D_MODEL ** -0.5),
        "w_down": nrm(ks[18], (DEPTH, D_FF, D_MODEL), D_FF ** -0.5),
        "g_post_ffn": gain(ks[19]),
    }


def reference(x_prompt, x_sample, cache_swa_k, cache_swa_v, cache_dil_k, cache_dil_v,
              cache_mem_k, cache_mem_v, mem_prompt, g_pre_mix, w_in, sinks, w_mem_kv, w_o,
              g_post_mix, g_pre_ffn, w_gate, w_up, w_down, g_post_ffn):
    t_p = x_prompt.shape[1]
    s_len = x_sample.shape[1]
    lb_a = cache_swa_k.shape[2]
    lb_b = cache_dil_k.shape[2]
    pos_p = jnp.arange(t_p)
    pos_s = PAST_LEN + jnp.arange(s_len)
    hp, hs = x_prompt, x_sample
    p_swa_k, p_swa_v, p_dil_k, p_dil_v, p_mem_k, p_mem_v = [], [], [], [], [], []
    s_swa_k, s_swa_v, s_dil_k, s_dil_v = [], [], [], []
    for l in range(DEPTH):
        u = rms_norm(hp, g_pre_mix[l])
        qa, ka, va, qb, kb, vb, qx = mix_inputs(u, w_in[l], pos_p)
        mk, mv = mem_kv(mem_prompt, w_mem_kv[l])
        oa = sink_combine(*banded_stats(qa, ka, va, SWA_WINDOW), sinks[l])
        ob = dilated_prompt(qb, kb, vb)
        ox = mem_attend(qx, mk, mv)
        hp = finish_layer(hp, oa, ob, ox, w_o[l], g_post_mix[l], g_pre_ffn[l],
                          w_gate[l], w_up[l], w_down[l], g_post_ffn[l])
        p_swa_k.append(ka[:, t_p - min(SWA_WINDOW, t_p):])
        p_swa_v.append(va[:, t_p - min(SWA_WINDOW, t_p):])
        p_dil_k.append(kb[:, t_p - min(DIL_MAX_WINDOW, t_p):])
        p_dil_v.append(vb[:, t_p - min(DIL_MAX_WINDOW, t_p):])
        p_mem_k.append(mk)
        p_mem_v.append(mv)

        u = rms_norm(hs, g_pre_mix[l])
        qa, ka, va, qb, kb, vb, qx = mix_inputs(u, w_in[l], pos_s)
        ka_all = jnp.concatenate([cache_swa_k[l], ka.astype(cache_swa_k.dtype)], axis=1)
        va_all = jnp.concatenate([cache_swa_v[l], va.astype(cache_swa_v.dtype)], axis=1)
        kb_all = jnp.concatenate([cache_dil_k[l], kb.astype(cache_dil_k.dtype)], axis=1)
        vb_all = jnp.concatenate([cache_dil_v[l], vb.astype(cache_dil_v.dtype)], axis=1)
        oa = sink_combine(*window_gather_stats(qa, ka_all, va_all, lb_a, SWA_WINDOW, 1), sinks[l])
        ob = dilated_sample(qb, kb_all, vb_all, lb_b)
        ox = mem_attend(qx, cache_mem_k[l], cache_mem_v[l])
        hs = finish_layer(hs, oa, ob, ox, w_o[l], g_post_mix[l], g_pre_ffn[l],
                          w_gate[l], w_up[l], w_down[l], g_post_ffn[l])
        s_swa_k.append(ka_all[:, s_len:])
        s_swa_v.append(va_all[:, s_len:])
        s_dil_k.append(kb_all[:, s_len:])
        s_dil_v.append(vb_all[:, s_len:])
    return (hp, hs,
            jnp.stack(p_swa_k), jnp.stack(p_swa_v), jnp.stack(p_dil_k), jnp.stack(p_dil_v),
            jnp.stack(p_mem_k), jnp.stack(p_mem_v),
            jnp.stack(s_swa_k), jnp.stack(s_swa_v), jnp.stack(s_dil_k), jnp.stack(s_dil_v))
```

```python
import functools

import numpy as np
import jax
import jax.numpy as jnp
from jax import lax
from jax.experimental import pallas as pl
from jax.experimental.pallas import tpu as pltpu

F32 = jnp.float32
BF16 = jnp.bfloat16

HEAD_DIM = 64
SWA_KV_HEADS = 2
SWA_GROUP = 3
SWA_WINDOW = 128
DIL_HEADS = 6
DIL_PATTERNS = ((128, 1), (512, 4), (2048, 16))
DIL_MAX_WINDOW = 2048
MEM_HEADS = 4
BLK = 128
ROPE_THETA = 10000.0
EPS = 1e-6
SCALE = HEAD_DIM ** -0.5
PAST_LEN = 16384

Q_A = SWA_KV_HEADS * SWA_GROUP * HEAD_DIM
KV_A = SWA_KV_HEADS * HEAD_DIM
Q_B = DIL_HEADS * HEAD_DIM
KV_B = DIL_HEADS * HEAD_DIM
Q_X = MEM_HEADS * HEAD_DIM
IN_SIZES = (Q_A, KV_A, KV_A, Q_B, KV_B, KV_B, Q_X)
IN_OFFS = tuple(int(v) for v in np.cumsum((0,) + IN_SIZES))

LANES = 128
VMEM_LIMIT = 56 * 1024 * 1024

SWA_HEAD_ORDER = (0, 3, 1, 4, 2, 5)


def _params(n_axes):
    return pltpu.CompilerParams(dimension_semantics=("arbitrary",) * n_axes,
                                vmem_limit_bytes=VMEM_LIMIT)


def _rms(x, g):
    return x * lax.rsqrt(jnp.mean(x * x, axis=-1, keepdims=True) + EPS) * g


def _lane_lo(shape):
    return lax.broadcasted_iota(jnp.int32, shape, len(shape) - 1) % LANES < HEAD_DIM


def _proj_kernel(x_ref, g_ref, w_ref, cos_ref, sin_ref, *out_refs, prompt, n_a, n_b):
    tm = x_ref.shape[0]
    u = _rms(x_ref[...], g_ref[...]).astype(BF16)
    cos = cos_ref[...]
    sin = sin_ref[...]
    first_half = lax.broadcasted_iota(jnp.int32, cos.shape, 1) % HEAD_DIM < HEAD_DIM // 2

    def proj(i):
        return jnp.dot(u, w_ref[:, IN_OFFS[i]:IN_OFFS[i + 1]], preferred_element_type=F32)

    def rope(z):
        tiles = []
        for s in range(z.shape[1] // LANES):
            zs = z[:, s * LANES:(s + 1) * LANES]
            partner = jnp.where(first_half, pltpu.roll(zs, LANES - HEAD_DIM // 2, 1),
                                pltpu.roll(zs, HEAD_DIM // 2, 1))
            tiles.append(zs * cos + partner * sin)
        return jnp.concatenate(tiles, axis=1) if len(tiles) > 1 else tiles[0]

    qa = rope(proj(0))
    ka = rope(proj(1))
    va = proj(2)
    qb = rope(proj(3))
    kb = rope(proj(4))
    vb = proj(5)
    qx = proj(6)
    qa_ref, qb_ref, qx_ref, ka_ref, va_ref, kb_ref, vb_ref = out_refs[:7]
    qa_ref[...] = (qa * SCALE).astype(qa_ref.dtype)
    qx_ref[...] = (qx * SCALE).astype(qx_ref.dtype)
    ka_ref[...] = ka.astype(ka_ref.dtype)
    va_ref[...] = va.astype(va_ref.dtype)
    if not prompt:
        qb_ref[...] = qb * SCALE
        kb_ref[...] = kb
        vb_ref[...] = vb
    else:
        for p in range(Q_B // LANES):
            cols = slice(p * LANES, (p + 1) * LANES)
            qb_ref[p] = qb[:, cols] * SCALE
            kb_ref[p] = kb[:, cols]
            vb_ref[p] = vb[:, cols]
        kat_ref, vat_ref, kbt_ref, vbt_ref = out_refs[7:]
        j = pl.program_id(1)
        nj = pl.num_programs(1)

        @pl.when(j >= nj - n_b // tm)
        def _():
            kbt_ref[...] = kb.T
            vbt_ref[...] = vb.T

        @pl.when(j == nj - 1)
        def _():
            kat_ref[...] = ka[tm - n_a:, :].T
            vat_ref[...] = va[tm - n_a:, :].T


def _proj(x, g, w_bf, cos, sin, *, tm, prompt, n_a=0, n_b=0):
    b, t, d = x.shape
    nj = t // tm
    n_rope = cos.shape[0] // tm
    tile = lambda c: pl.BlockSpec((None, tm, c), lambda i, j: (i, j, 0))
    rope = pl.BlockSpec((tm, LANES), lambda i, j: (j % n_rope, 0))
    dt = BF16 if prompt else F32
    widths = (Q_A, Q_B, Q_X, KV_A, KV_A, KV_B, KV_B)
    out_specs = [tile(c) for c in widths]
    out_shape = [jax.ShapeDtypeStruct((b, t, c), dt) for c in widths]
    if prompt:
        assert n_a <= tm and n_a % LANES == 0 and n_b % tm == 0 and n_b <= t
        n_pairs = Q_B // LANES
        for i in (1, 5, 6):
            out_specs[i] = pl.BlockSpec((None, n_pairs, tm, LANES), lambda i, j: (i, 0, j, 0))
            out_shape[i] = jax.ShapeDtypeStruct((b, n_pairs, t, LANES), F32)
        first_b = nj - n_b // tm
        out_specs += [pl.BlockSpec((None, KV_A, n_a), lambda i, j: (i, 0, 0))] * 2
        out_specs += [pl.BlockSpec((None, KV_B, tm), lambda i, j: (i, 0, jnp.maximum(j - first_b, 0)))] * 2
        out_shape += [jax.ShapeDtypeStruct((b, KV_A, n_a), F32)] * 2 + [jax.ShapeDtypeStruct((b, KV_B, n_b), F32)] * 2
    return pl.pallas_call(
        functools.partial(_proj_kernel, prompt=prompt, n_a=n_a, n_b=n_b),
        grid=(b, nj),
        in_specs=[tile(d), pl.BlockSpec((1, d), lambda i, j: (0, 0)),
                  pl.BlockSpec(w_bf.shape, lambda i, j: (0, 0), pipeline_mode=pl.Buffered(1)),
                  rope, rope],
        out_specs=out_specs,
        out_shape=out_shape,
        compiler_params=_params(2),
        name="proj_prompt" if prompt else "proj_sample",
    )(x, g, w_bf, cos, sin)


def _memkv_kernel(mem_ref, w_ref, mkt_ref, mvt_ref, mkh_ref, mvh_ref):
    z = jnp.dot(mem_ref[...].astype(BF16), w_ref[...], preferred_element_type=F32)
    mk = z[:, :Q_X]
    mv = z[:, Q_X:]
    mkt_ref[...] = mk.T
    mvt_ref[...] = mv.T
    mkh_ref[...] = mk.astype(BF16)
    mvh_ref[...] = mv.astype(BF16)


def _memkv(mem, w_bf):
    b, m, d = mem.shape
    blk = lambda r, c: pl.BlockSpec((None, r, c), lambda i: (i, 0, 0))
    return pl.pallas_call(
        _memkv_kernel,
        grid=(b,),
        in_specs=[blk(m, d), pl.BlockSpec(w_bf.shape, lambda i: (0, 0))],
        out_specs=[blk(Q_X, m)] * 2 + [blk(m, Q_X)] * 2,
        out_shape=[jax.ShapeDtypeStruct((b, Q_X, m), F32)] * 2 + [jax.ShapeDtypeStruct((b, m, Q_X), BF16)] * 2,
        compiler_params=_params(1),
        name="memkv",
    )(mem, w_bf)


def _pair_attend(q, kk, vv, mask):
    tq = q.shape[0]
    zero = jnp.zeros_like(q)
    q_lo = _lane_lo(q.shape)
    q2 = jnp.concatenate([jnp.where(q_lo, q, zero), jnp.where(q_lo, zero, q)], axis=0)
    s = lax.dot_general(q2, kk, (((1,), (1,)), ((), ())), preferred_element_type=F32)
    stats = []
    for h in range(2):
        sh = s[h * tq:(h + 1) * tq]
        if mask is not None:
            sh = jnp.where(mask, sh, -jnp.inf)
        m = jnp.max(sh, axis=1, keepdims=True)
        p = jnp.exp(sh - m)
        stats.append((m, jnp.sum(p, axis=1, keepdims=True), p.astype(BF16)))
    v_lo = _lane_lo(vv.shape)
    vzero = jnp.zeros_like(vv)
    o = (jnp.dot(stats[0][2], jnp.where(v_lo, vv, vzero), preferred_element_type=F32)
         + jnp.dot(stats[1][2], jnp.where(v_lo, vzero, vv), preferred_element_type=F32))
    o_lo = _lane_lo(o.shape)
    m = jnp.where(o_lo, stats[0][0], stats[1][0])
    l = jnp.where(o_lo, stats[0][1], stats[1][1])
    return o, m, l


def _band_mask(has_prev):
    qi = lax.broadcasted_iota(jnp.int32, (BLK, 2 * BLK), 0)
    c = lax.broadcasted_iota(jnp.int32, (BLK, 2 * BLK), 1)
    first_prev = jnp.where(has_prev, 0, BLK)
    in_prev = (c < BLK) & (c >= qi) & (c >= first_prev)
    in_cur = (c >= BLK) & (c - BLK <= qi)
    return in_prev | in_cur


def _band_rows(j, i, n_blk):
    n = j * n_blk + i
    r0 = pl.multiple_of(i * BLK, BLK)
    k_cur = pl.multiple_of(n * BLK, BLK)
    k_prev = pl.multiple_of(jnp.maximum(n - 1, 0) * BLK, BLK)
    return n, r0, k_prev, k_cur


def _swa_mem_kernel(qa_ref, ka_ref, va_ref, qx_ref, mk_ref, mv_ref, sink_ref, oa_ref, ox_ref, *, qt):
    j = pl.program_id(1)
    n_blk = qt // BLK

    def body(i, carry):
        n, r0, k_prev, k_cur = _band_rows(j, i, n_blk)
        rows = pl.ds(r0, BLK)
        mask = _band_mask(n > 0)
        kk = jnp.concatenate([ka_ref[pl.ds(k_prev, BLK), :], ka_ref[pl.ds(k_cur, BLK), :]], axis=0)
        vv = jnp.concatenate([va_ref[pl.ds(k_prev, BLK), :], va_ref[pl.ds(k_cur, BLK), :]], axis=0)
        for p in range(Q_A // LANES):
            cols = slice(p * LANES, (p + 1) * LANES)
            o, m, l = _pair_attend(qa_ref[rows, cols], kk, vv, mask)
            sink = sink_ref[:, cols]
            mm = jnp.maximum(m, sink)
            a = jnp.exp(m - mm)
            den = l * a + jnp.exp(sink - mm)
            oa_ref[rows, cols] = (o * (a / den)).astype(oa_ref.dtype)
        for p in range(Q_X // LANES):
            cols = slice(p * LANES, (p + 1) * LANES)
            o, m, l = _pair_attend(qx_ref[rows, cols], mk_ref[:, cols], mv_ref[:, cols], None)
            ox_ref[rows, cols] = (o / l).astype(ox_ref.dtype)
        return carry

    lax.fori_loop(0, n_blk, body, 0)


def _swa_mem(qa, ka, va, qx, mk, mv, sink_row, *, qt):
    b, t, _ = qa.shape
    m = mk.shape[1]
    qtile = lambda c: pl.BlockSpec((None, qt, c), lambda i, j: (i, j, 0))
    whole = lambda r, c: pl.BlockSpec((None, r, c), lambda i, j: (i, 0, 0))
    return pl.pallas_call(
        functools.partial(_swa_mem_kernel, qt=qt),
        grid=(b, t // qt),
        in_specs=[qtile(Q_A), whole(t, KV_A), whole(t, KV_A), qtile(Q_X), whole(m, Q_X), whole(m, Q_X),
                  pl.BlockSpec((1, Q_A), lambda i, j: (0, 0))],
        out_specs=[qtile(Q_A), qtile(Q_X)],
        out_shape=[jax.ShapeDtypeStruct((b, t, Q_A), BF16), jax.ShapeDtypeStruct((b, t, Q_X), BF16)],
        compiler_params=_params(2),
        name="swa_mem",
    )(qa, ka, va, qx, mk, mv, sink_row)


def _dil_kernel(q_ref, k_ref, v_ref, ob_ref, acc_o, acc_m, acc_l, *, qt):
    j = pl.program_id(1)
    patterns = sorted(DIL_PATTERNS, key=lambda wr: -wr[1])
    for p in range(Q_B // LANES):
        for idx, (w, r) in enumerate(patterns):
            assert w // r == BLK and qt % (r * BLK) == 0
            n_blk = qt // (r * BLK)

            def body(step, carry, r=r, n_blk=n_blk, first=(idx == 0)):
                a = lax.div(step, n_blk)
                i = lax.rem(step, n_blk)
                n = j * n_blk + i

                def rows(blk):
                    if r == 1:
                        return pl.ds(pl.multiple_of(blk * BLK, BLK), BLK)
                    return pl.ds(blk * (BLK * r) + a, BLK, stride=r)

                prev = jnp.maximum(n - 1, 0)
                q = q_ref[p, rows(i), :].astype(BF16)
                kk = jnp.concatenate([k_ref[p, rows(prev), :], k_ref[p, rows(n), :]], axis=0).astype(BF16)
                vv = jnp.concatenate([v_ref[p, rows(prev), :], v_ref[p, rows(n), :]], axis=0).astype(BF16)
                o, m, l = _pair_attend(q, kk, vv, _band_mask(n > 0))
                if not first:
                    m_old = acc_m[rows(i), :]
                    m_new = jnp.maximum(m_old, m)
                    e_old = jnp.exp(m_old - m_new)
                    e = jnp.exp(m - m_new)
                    o = acc_o[rows(i), :] * e_old + o * e
                    l = acc_l[rows(i), :] * e_old + l * e
                    m = m_new
                acc_o[rows(i), :] = o
                acc_m[rows(i), :] = m
                acc_l[rows(i), :] = l
                return carry

            lax.fori_loop(0, r * n_blk, body, 0)
        ob_ref[:, p * LANES:(p + 1) * LANES] = (acc_o[...] / acc_l[...]).astype(ob_ref.dtype)


def _dil(q, k, v, *, qt):
    b, n_pairs, t, _ = q.shape
    return pl.pallas_call(
        functools.partial(_dil_kernel, qt=qt),
        grid=(b, t // qt),
        in_specs=[pl.BlockSpec((None, n_pairs, qt, LANES), lambda i, j: (i, 0, j, 0)),
                  pl.BlockSpec((None, n_pairs, t, LANES), lambda i, j: (i, 0, 0, 0)),
                  pl.BlockSpec((None, n_pairs, t, LANES), lambda i, j: (i, 0, 0, 0))],
        out_specs=pl.BlockSpec((None, qt, n_pairs * LANES), lambda i, j: (i, j, 0)),
        out_shape=jax.ShapeDtypeStruct((b, t, n_pairs * LANES), BF16),
        scratch_shapes=[pltpu.VMEM((qt, LANES), F32)] * 3,
        compiler_params=_params(2),
        name="dil",
    )(q, k, v)


def _sample_kernel(qa_ref, kan_ref, van_ref, qb_ref, kbn_ref, vbn_ref, qx_ref,
                   cak_ref, cav_ref, cbk_ref, cbv_ref, cmk_ref, cmv_ref,
                   sink_ref, amask_ref, bcnt_ref,
                   oa_ref, ob_ref, ox_ref, sak_ref, sav_ref, sbk_ref, sbv_ref,
                   ka_s, va_s, kb_s, vb_s, *, s_len):
    def shift_and_stage(cache_ref, new_ref, out_ref, stage):
        c, lb = cache_ref.shape
        new = new_ref[...]
        new_t = jnp.concatenate([new, jnp.zeros((LANES - s_len, c), F32)], axis=0).T
        cache = cache_ref[...]
        stage[:, 0:lb] = cache.astype(BF16)
        stage[:, lb:] = new_t.astype(BF16)
        shifted = pltpu.roll(cache, lb - s_len, 1)
        if lb > LANES:
            out_ref[:, 0:lb - LANES] = shifted[:, 0:lb - LANES]
        keep = lax.broadcasted_iota(jnp.int32, (c, LANES), 1) < LANES - s_len
        out_ref[:, lb - LANES:] = jnp.where(keep, shifted[:, lb - LANES:], pltpu.roll(new_t, LANES - s_len, 1))

    shift_and_stage(cak_ref, kan_ref, sak_ref, ka_s)
    shift_and_stage(cav_ref, van_ref, sav_ref, va_s)
    shift_and_stage(cbk_ref, kbn_ref, sbk_ref, kb_s)
    shift_and_stage(cbv_ref, vbn_ref, sbv_ref, vb_s)

    def scores(qblk, keys_t):
        return jnp.dot(qblk.astype(BF16), keys_t, preferred_element_type=F32)

    def weighted(p, values_t):
        return lax.dot_general(p.astype(BF16), values_t, (((1,), (1,)), ((), ())), preferred_element_type=F32)

    def stack_heads(q, n_heads):
        return jnp.concatenate([q] * n_heads, axis=0)

    def own_head(shape):
        row_h = lax.broadcasted_iota(jnp.int32, shape, 0) // s_len
        col_h = lax.broadcasted_iota(jnp.int32, shape, 1) // HEAD_DIM
        return row_h == col_h

    def fold_heads(o_all, n_heads):
        o_all = jnp.where(own_head(o_all.shape), o_all, 0.0)
        o = o_all[0:s_len]
        for h in range(1, n_heads):
            o = o + o_all[h * s_len:(h + 1) * s_len]
        return o

    qa = qa_ref[...]
    lo = _lane_lo((s_len, LANES))
    pieces = []
    for p in range(Q_A // LANES):
        tile = qa[:, p * LANES:(p + 1) * LANES]
        pieces += [jnp.where(lo, tile, 0.0), jnp.where(lo, 0.0, tile)]
    s = scores(jnp.concatenate(pieces, axis=0), ka_s[...])
    s = jnp.where(amask_ref[...] > 0.0, s, -jnp.inf)
    m = jnp.max(s, axis=1, keepdims=True)
    p_ = jnp.exp(s - m)
    l = jnp.sum(p_, axis=1, keepdims=True)
    o = weighted(p_, va_s[...])
    sink = sink_ref[...]
    mm = jnp.maximum(m, sink)
    a = jnp.exp(m - mm)
    o = o * (a / (l * a + jnp.exp(sink - mm)))
    oa_ref[...] = jnp.concatenate(
        [jnp.where(lo, o[2 * p * s_len:(2 * p + 1) * s_len], o[(2 * p + 1) * s_len:(2 * p + 2) * s_len])
         for p in range(Q_A // LANES)], axis=1)

    qb = stack_heads(qb_ref[...], DIL_HEADS)
    s = scores(jnp.where(own_head(qb.shape), qb, 0.0), kb_s[...])
    cnt = bcnt_ref[...]
    s = jnp.where(cnt > 0.0, s, -jnp.inf)
    m = jnp.max(s, axis=1, keepdims=True)
    p_ = cnt * jnp.exp(s - m)
    l = jnp.sum(p_, axis=1, keepdims=True)
    ob_ref[...] = fold_heads(weighted(p_, vb_s[...]) / l, DIL_HEADS)

    qx = stack_heads(qx_ref[...], MEM_HEADS)
    s = scores(jnp.where(own_head(qx.shape), qx, 0.0), cmk_ref[...].astype(BF16))
    m = jnp.max(s, axis=1, keepdims=True)
    p_ = jnp.exp(s - m)
    l = jnp.sum(p_, axis=1, keepdims=True)
    ox_ref[...] = fold_heads(weighted(p_, cmv_ref[...].astype(BF16)) / l, MEM_HEADS)


def _sample_tables(s_len, lb_a, lb_b, na, nb):
    i = np.arange(s_len)[:, None]
    dist_a = lb_a + i - np.arange(na)[None, :]
    amask = ((dist_a >= 0) & (dist_a <= SWA_WINDOW)).astype(np.float32)
    dist_b = lb_b + i - np.arange(nb)[None, :]
    cnt = np.zeros(dist_b.shape, np.float32)
    for w, r in DIL_PATTERNS:
        cnt += ((dist_b >= 0) & (dist_b <= w) & (dist_b % r == 0)).astype(np.float32)
    return np.tile(amask, (SWA_KV_HEADS * SWA_GROUP, 1)), np.tile(cnt, (DIL_HEADS, 1))


def _sample_attend(qa, kan, van, qb, kbn, vbn, qx, cak, cav, cbk, cbv, cmk, cmv, sink_rows, *, s_len):
    nbat, _, lb_a = cak.shape
    lb_b = cbk.shape[2]
    mtok = cmk.shape[2]
    assert lb_a >= SWA_WINDOW and lb_b >= DIL_MAX_WINDOW, "every window position must lie inside the cache"
    assert lb_a % LANES == 0 and lb_b % LANES == 0 and s_len <= LANES
    na = lb_a + LANES
    nb = lb_b + LANES
    amask, bcnt = _sample_tables(s_len, lb_a, lb_b, na, nb)
    tok = lambda c: pl.BlockSpec((s_len, c), lambda n: (n, 0))
    per_b = lambda r, c: pl.BlockSpec((None, r, c), lambda n: (n, 0, 0))
    const = lambda a: pl.BlockSpec(a.shape, lambda n: (0, 0))
    n_tok = nbat * s_len
    return pl.pallas_call(
        functools.partial(_sample_kernel, s_len=s_len),
        grid=(nbat,),
        in_specs=[tok(Q_A), tok(KV_A), tok(KV_A), tok(Q_B), tok(KV_B), tok(KV_B), tok(Q_X),
                  per_b(KV_A, lb_a), per_b(KV_A, lb_a), per_b(KV_B, lb_b), per_b(KV_B, lb_b),
                  per_b(Q_X, mtok), per_b(Q_X, mtok),
                  const(sink_rows), const(amask), const(bcnt)],
        out_specs=[tok(Q_A), tok(Q_B), tok(Q_X),
                   per_b(KV_A, lb_a), per_b(KV_A, lb_a), per_b(KV_B, lb_b), per_b(KV_B, lb_b)],
        out_shape=[jax.ShapeDtypeStruct((n_tok, Q_A), F32), jax.ShapeDtypeStruct((n_tok, Q_B), F32),
                   jax.ShapeDtypeStruct((n_tok, Q_X), F32),
                   jax.ShapeDtypeStruct(cak.shape, F32), jax.ShapeDtypeStruct(cav.shape, F32),
                   jax.ShapeDtypeStruct(cbk.shape, F32), jax.ShapeDtypeStruct(cbv.shape, F32)],
        scratch_shapes=[pltpu.VMEM((KV_A, na), BF16), pltpu.VMEM((KV_A, na), BF16),
                        pltpu.VMEM((KV_B, nb), BF16), pltpu.VMEM((KV_B, nb), BF16)],
        compiler_params=_params(1),
        name="sample_attend",
    )(qa, kan, van, qb, kbn, vbn, qx, cak, cav, cbk, cbv, cmk, cmv,
      sink_rows, jnp.asarray(amask), jnp.asarray(bcnt))


def _finish_kernel(x_ref, oa_ref, ob_ref, ox_ref, wo_ref, g1_ref, g2_ref, wg_ref, wu_ref, wd_ref, g3_ref, out_ref,
                   *, ff_chunk):
    cat = jnp.concatenate([oa_ref[...].astype(BF16), ob_ref[...].astype(BF16), ox_ref[...].astype(BF16)], axis=1)
    x = x_ref[...]
    x = x + _rms(jnp.dot(cat, wo_ref[...], preferred_element_type=F32), g1_ref[...])
    h = _rms(x, g2_ref[...]).astype(BF16)
    d_ff = wg_ref.shape[1]
    f = 0.0
    for c0 in range(0, d_ff, ff_chunk):
        cols = slice(c0, c0 + ff_chunk)
        g = jnp.dot(h, wg_ref[:, cols], preferred_element_type=F32)
        u = jnp.dot(h, wu_ref[:, cols], preferred_element_type=F32)
        act = (g * (1.0 / (1.0 + jnp.exp(-g))) * u).astype(BF16)
        f = f + jnp.dot(act, wd_ref[cols, :], preferred_element_type=F32)
    out_ref[...] = x + _rms(f, g3_ref[...])


def _finish(x, oa, ob, ox, wo, g1, g2, wg, wu, wd, g3, *, tm, ff_chunk):
    n, d = x.shape
    row = lambda c: pl.BlockSpec((tm, c), lambda i: (i, 0))
    const = lambda a: pl.BlockSpec(a.shape, lambda i: (0, 0), pipeline_mode=pl.Buffered(1))
    return pl.pallas_call(
        functools.partial(_finish_kernel, ff_chunk=ff_chunk),
        grid=(n // tm,),
        in_specs=[row(d), row(Q_A), row(Q_B), row(Q_X)] + [const(a) for a in (wo, g1, g2, wg, wu, wd, g3)],
        out_specs=row(d),
        out_shape=jax.ShapeDtypeStruct((n, d), F32),
        compiler_params=_params(1),
        name="finish",
    )(x, oa, ob, ox, wo, g1, g2, wg, wu, wd, g3)


def _rope_tables(pos):
    half = HEAD_DIM // 2
    inv = ROPE_THETA ** (-jnp.arange(half, dtype=F32) / half)
    ang = pos.astype(F32)[:, None] * inv[None, :]
    cos = jnp.tile(jnp.cos(ang), (1, LANES // half))
    sin = jnp.tile(jnp.sin(ang), (1, LANES // half))
    first_half = (jnp.arange(LANES) % HEAD_DIM) < half
    return cos, jnp.where(first_half[None, :], -sin, sin)


def _swa_head_perm():
    return np.concatenate([np.arange(h * HEAD_DIM, (h + 1) * HEAD_DIM) for h in SWA_HEAD_ORDER])


def _seq_minor(c):
    b, n, h, dh = c.shape
    return jnp.transpose(c, (0, 2, 3, 1)).reshape(b, h * dh, n)


def _seq_major(a, heads):
    b, c, n = a.shape
    return jnp.transpose(a.reshape(b, heads, c // heads, n), (0, 3, 1, 2))


def kernel(x_prompt, x_sample, cache_swa_k, cache_swa_v, cache_dil_k, cache_dil_v, cache_mem_k, cache_mem_v,
           mem_prompt, g_pre_mix, w_in, sinks, w_mem_kv, w_o, g_post_mix, g_pre_ffn, w_gate, w_up, w_down,
           g_post_ffn):
    depth = w_in.shape[0]
    bp, t_p, d = x_prompt.shape
    bs, s_len, _ = x_sample.shape
    d_ff = w_gate.shape[2]
    perm = _swa_head_perm()
    cos_p, sin_p = _rope_tables(jnp.arange(t_p))
    cos_s, sin_s = _rope_tables(PAST_LEN + jnp.arange(s_len))
    tm_s = 256
    cos_s = jnp.tile(cos_s, (tm_s // s_len, 1))
    sin_s = jnp.tile(sin_s, (tm_s // s_len, 1))
    ff_chunk = d_ff // 2 if (d_ff // 2) % LANES == 0 else d_ff
    n_a = min(SWA_WINDOW, t_p)
    n_b = min(DIL_MAX_WINDOW, t_p)

    hp = x_prompt
    hs = x_sample.reshape(1, bs * s_len, d)
    outs = [[] for _ in range(10)]
    for l in range(depth):
        w_in_l = w_in[l]
        w_in_bf = jnp.concatenate([w_in_l[:, :Q_A][:, perm], w_in_l[:, Q_A:]], axis=1).astype(BF16)
        w_o_bf = jnp.concatenate([w_o[l][:Q_A][perm], w_o[l][Q_A:]], axis=0).astype(BF16)
        w_mem_bf = w_mem_kv[l].astype(BF16)
        wg, wu, wd = w_gate[l].astype(BF16), w_up[l].astype(BF16), w_down[l].astype(BF16)
        g0, g1, g2, g3 = (g[l][None, :] for g in (g_pre_mix, g_post_mix, g_pre_ffn, g_post_ffn))
        sink_heads = sinks[l].reshape(-1)[np.array(SWA_HEAD_ORDER)].astype(F32)
        sink_row = jnp.repeat(sink_heads, HEAD_DIM)[None, :]
        sink_rows = jnp.repeat(sink_heads, s_len)[:, None]

        qa, qb, qx, kah, vah, kbf, vbf, kat, vat, kbt, vbt = _proj(
            hp, g0, w_in_bf, cos_p, sin_p, tm=512, prompt=True, n_a=n_a, n_b=n_b)
        mkt, mvt, mkh, mvh = _memkv(mem_prompt, w_mem_bf)
        oa, ox = _swa_mem(qa, kah, vah, qx, mkh, mvh, sink_row, qt=1024)
        ob = _dil(qb, kbf, vbf, qt=2048)
        flat = lambda a: a.reshape(bp * t_p, a.shape[-1])
        hp = _finish(flat(hp), flat(oa), flat(ob), flat(ox), w_o_bf, g1, g2, wg, wu, wd, g3,
                     tm=512, ff_chunk=ff_chunk).reshape(bp, t_p, d)
        outs[0].append(_seq_major(kat, SWA_KV_HEADS))
        outs[1].append(_seq_major(vat, SWA_KV_HEADS))
        outs[2].append(_seq_major(kbt, DIL_HEADS))
        outs[3].append(_seq_major(vbt, DIL_HEADS))
        outs[4].append(_seq_major(mkt, MEM_HEADS))
        outs[5].append(_seq_major(mvt, MEM_HEADS))

        qa, qb, qx, kan, van, kbn, vbn = (a[0] for a in _proj(
            hs, g0, w_in_bf, cos_s, sin_s, tm=tm_s, prompt=False))
        oa, ob, ox, sak, sav, sbk, sbv = _sample_attend(
            qa, kan, van, qb, kbn, vbn, qx,
            _seq_minor(cache_swa_k[l]), _seq_minor(cache_swa_v[l]),
            _seq_minor(cache_dil_k[l]), _seq_minor(cache_dil_v[l]),
            _seq_minor(cache_mem_k[l]), _seq_minor(cache_mem_v[l]),
            sink_rows, s_len=s_len)
        hs = _finish(hs[0], oa, ob, ox, w_o_bf, g1, g2, wg, wu, wd, g3,
                     tm=256, ff_chunk=ff_chunk)[None]
        outs[6].append(_seq_major(sak, SWA_KV_HEADS))
        outs[7].append(_seq_major(sav, SWA_KV_HEADS))
        outs[8].append(_seq_major(sbk, DIL_HEADS))
        outs[9].append(_seq_major(sbv, DIL_HEADS))

    return (hp, hs.reshape(bs, s_len, d)) + tuple(jnp.stack(o) for o in outs)
```

```python
import functools

import numpy as np
import jax
import jax.numpy as jnp
from jax import lax
from jax.experimental import pallas as pl
from jax.experimental.pallas import tpu as pltpu

F32 = jnp.float32
BF16 = jnp.bfloat16

HEAD_DIM = 64
SWA_KV_HEADS = 2
SWA_GROUP = 3
SWA_WINDOW = 128
DIL_HEADS = 6
DIL_PATTERNS = ((128, 1), (512, 4), (2048, 16))
DIL_MAX_WINDOW = 2048
MEM_HEADS = 4
BLK = 128
ROPE_THETA = 10000.0
EPS = 1e-6
SCALE = HEAD_DIM ** -0.5
PAST_LEN = 16384

Q_A = SWA_KV_HEADS * SWA_GROUP * HEAD_DIM
KV_A = SWA_KV_HEADS * HEAD_DIM
Q_B = DIL_HEADS * HEAD_DIM
KV_B = DIL_HEADS * HEAD_DIM
Q_X = MEM_HEADS * HEAD_DIM
IN_SIZES = (Q_A, KV_A, KV_A, Q_B, KV_B, KV_B, Q_X)
IN_OFFS = tuple(int(v) for v in np.cumsum((0,) + IN_SIZES))

LANES = 128
VMEM_LIMIT = 56 * 1024 * 1024

SWA_HEAD_ORDER = (0, 3, 1, 4, 2, 5)


def _params(n_axes):
    return pltpu.CompilerParams(dimension_semantics=("arbitrary",) * n_axes,
                                vmem_limit_bytes=VMEM_LIMIT)


def _rms(x, g):
    return x * lax.rsqrt(jnp.mean(x * x, axis=-1, keepdims=True) + EPS) * g


def _lane_lo(shape):
    return lax.broadcasted_iota(jnp.int32, shape, len(shape) - 1) % LANES < HEAD_DIM


def _proj_kernel(x_ref, g_ref, w_ref, cos_ref, sin_ref, *out_refs, prompt, n_a, n_b):
    tm = x_ref.shape[0]
    u = _rms(x_ref[...], g_ref[...]).astype(BF16)
    cos = cos_ref[...]
    sin = sin_ref[...]
    first_half = lax.broadcasted_iota(jnp.int32, cos.shape, 1) % HEAD_DIM < HEAD_DIM // 2

    def proj(i):
        return jnp.dot(u, w_ref[:, IN_OFFS[i]:IN_OFFS[i + 1]], preferred_element_type=F32)

    def rope(z):
        tiles = []
        for s in range(z.shape[1] // LANES):
            zs = z[:, s * LANES:(s + 1) * LANES]
            partner = jnp.where(first_half, pltpu.roll(zs, LANES - HEAD_DIM // 2, 1),
                                pltpu.roll(zs, HEAD_DIM // 2, 1))
            tiles.append(zs * cos + partner * sin)
        return jnp.concatenate(tiles, axis=1) if len(tiles) > 1 else tiles[0]

    qa = rope(proj(0))
    ka = rope(proj(1))
    va = proj(2)
    qb = rope(proj(3))
    kb = rope(proj(4))
    vb = proj(5)
    qx = proj(6)
    qa_ref, qb_ref, qx_ref, ka_ref, va_ref, kb_ref, vb_ref = out_refs[:7]
    qa_ref[...] = (qa * SCALE).astype(qa_ref.dtype)
    qx_ref[...] = (qx * SCALE).astype(qx_ref.dtype)
    ka_ref[...] = ka.astype(ka_ref.dtype)
    va_ref[...] = va.astype(va_ref.dtype)
    if not prompt:
        qb_ref[...] = qb * SCALE
        kb_ref[...] = kb
        vb_ref[...] = vb
    else:
        for p in range(Q_B // LANES):
            cols = slice(p * LANES, (p + 1) * LANES)
            qb_ref[p] = qb[:, cols] * SCALE
            kb_ref[p] = kb[:, cols]
            vb_ref[p] = vb[:, cols]
        kat_ref, vat_ref, kbt_ref, vbt_ref = out_refs[7:]
        j = pl.program_id(1)
        nj = pl.num_programs(1)

        @pl.when(j >= nj - n_b // tm)
        def _():
            kbt_ref[...] = kb.T
            vbt_ref[...] = vb.T

        @pl.when(j == nj - 1)
        def _():
            kat_ref[...] = ka[tm - n_a:, :].T
            vat_ref[...] = va[tm - n_a:, :].T


def _proj(x, g, w_bf, cos, sin, *, tm, prompt, n_a=0, n_b=0):
    b, t, d = x.shape
    nj = t // tm
    n_rope = cos.shape[0] // tm
    tile = lambda c: pl.BlockSpec((None, tm, c), lambda i, j: (i, j, 0))
    rope = pl.BlockSpec((tm, LANES), lambda i, j: (j % n_rope, 0))
    dt = BF16 if prompt else F32
    widths = (Q_A, Q_B, Q_X, KV_A, KV_A, KV_B, KV_B)
    out_specs = [tile(c) for c in widths]
    out_shape = [jax.ShapeDtypeStruct((b, t, c), dt) for c in widths]
    if prompt:
        assert n_a <= tm and n_a % LANES == 0 and n_b % tm == 0 and n_b <= t
        n_pairs = Q_B // LANES
        for i in (1, 5, 6):
            out_specs[i] = pl.BlockSpec((None, n_pairs, tm, LANES), lambda i, j: (i, 0, j, 0))
            out_shape[i] = jax.ShapeDtypeStruct((b, n_pairs, t, LANES), F32)
        first_b = nj - n_b // tm
        out_specs += [pl.BlockSpec((None, KV_A, n_a), lambda i, j: (i, 0, 0))] * 2
        out_specs += [pl.BlockSpec((None, KV_B, tm), lambda i, j: (i, 0, jnp.maximum(j - first_b, 0)))] * 2
        out_shape += [jax.ShapeDtypeStruct((b, KV_A, n_a), F32)] * 2 + [jax.ShapeDtypeStruct((b, KV_B, n_b), F32)] * 2
    return pl.pallas_call(
        functools.partial(_proj_kernel, prompt=prompt, n_a=n_a, n_b=n_b),
        grid=(b, nj),
        in_specs=[tile(d), pl.BlockSpec((1, d), lambda i, j: (0, 0)),
                  pl.BlockSpec(w_bf.shape, lambda i, j: (0, 0), pipeline_mode=pl.Buffered(1)),
                  rope, rope],
        out_specs=out_specs,
        out_shape=out_shape,
        compiler_params=_params(2),
        name="proj_prompt" if prompt else "proj_sample",
    )(x, g, w_bf, cos, sin)


def _memkv_kernel(mem_ref, w_ref, mkt_ref, mvt_ref, mkh_ref, mvh_ref):
    z = jnp.dot(mem_ref[...].astype(BF16), w_ref[...], preferred_element_type=F32)
    mk = z[:, :Q_X]
    mv = z[:, Q_X:]
    mkt_ref[...] = mk.T
    mvt_ref[...] = mv.T
    mkh_ref[...] = mk.astype(BF16)
    mvh_ref[...] = mv.astype(BF16)


def _memkv(mem, w_bf):
    b, m, d = mem.shape
    blk = lambda r, c: pl.BlockSpec((None, r, c), lambda i: (i, 0, 0))
    return pl.pallas_call(
        _memkv_kernel,
        grid=(b,),
        in_specs=[blk(m, d), pl.BlockSpec(w_bf.shape, lambda i: (0, 0))],
        out_specs=[blk(Q_X, m)] * 2 + [blk(m, Q_X)] * 2,
        out_shape=[jax.ShapeDtypeStruct((b, Q_X, m), F32)] * 2 + [jax.ShapeDtypeStruct((b, m, Q_X), BF16)] * 2,
        compiler_params=_params(1),
        name="memkv",
    )(mem, w_bf)


def _pair_attend(q, kk, vv, bias):
    tq = q.shape[0]
    zero = jnp.zeros_like(q)
    q_lo = _lane_lo(q.shape)
    q2 = jnp.concatenate([jnp.where(q_lo, q, zero), jnp.where(q_lo, zero, q)], axis=0)
    s = lax.dot_general(q2, kk, (((1,), (1,)), ((), ())), preferred_element_type=F32)
    stats = []
    for h in range(2):
        sh = s[h * tq:(h + 1) * tq]
        if bias is not None:
            sh = sh + bias
        m = jnp.max(sh, axis=1, keepdims=True)
        p = jnp.exp(sh - m)
        stats.append((m, jnp.sum(p, axis=1, keepdims=True), p.astype(BF16)))
    v_lo = _lane_lo(vv.shape)
    vzero = jnp.zeros_like(vv)
    o = (jnp.dot(stats[0][2], jnp.where(v_lo, vv, vzero), preferred_element_type=F32)
         + jnp.dot(stats[1][2], jnp.where(v_lo, vzero, vv), preferred_element_type=F32))
    o_lo = _lane_lo(o.shape)
    m = jnp.where(o_lo, stats[0][0], stats[1][0])
    l = jnp.where(o_lo, stats[0][1], stats[1][1])
    return o, m, l


def _band_bias():
    qi = np.arange(BLK)[:, None]
    c = np.arange(2 * BLK)[None, :]
    in_prev = (c < BLK) & (c >= qi)
    in_cur = (c >= BLK) & (c - BLK <= qi)
    allowed = np.stack([in_cur, in_prev | in_cur])
    return np.where(allowed, 0.0, -np.inf).astype(np.float32)


def _band_rows(j, i, n_blk):
    n = j * n_blk + i
    r0 = pl.multiple_of(i * BLK, BLK)
    k_cur = pl.multiple_of(n * BLK, BLK)
    k_prev = pl.multiple_of(jnp.maximum(n - 1, 0) * BLK, BLK)
    return n, r0, k_prev, k_cur


def _swa_mem_kernel(qa_ref, ka_ref, va_ref, qx_ref, mk_ref, mv_ref, sink_ref, bias_ref, oa_ref, ox_ref, *, qt):
    j = pl.program_id(1)
    n_blk = qt // BLK

    def body(i, carry):
        n, r0, k_prev, k_cur = _band_rows(j, i, n_blk)
        rows = pl.ds(r0, BLK)
        mask = bias_ref[jnp.minimum(n, 1)]
        kk = jnp.concatenate([ka_ref[pl.ds(k_prev, BLK), :], ka_ref[pl.ds(k_cur, BLK), :]], axis=0)
        vv = jnp.concatenate([va_ref[pl.ds(k_prev, BLK), :], va_ref[pl.ds(k_cur, BLK), :]], axis=0)
        for p in range(Q_A // LANES):
            cols = slice(p * LANES, (p + 1) * LANES)
            o, m, l = _pair_attend(qa_ref[rows, cols], kk, vv, mask)
            sink = sink_ref[:, cols]
            mm = jnp.maximum(m, sink)
            a = jnp.exp(m - mm)
            den = l * a + jnp.exp(sink - mm)
            oa_ref[rows, cols] = (o * (a / den)).astype(oa_ref.dtype)
        for p in range(Q_X // LANES):
            cols = slice(p * LANES, (p + 1) * LANES)
            o, m, l = _pair_attend(qx_ref[rows, cols], mk_ref[:, cols], mv_ref[:, cols], None)
            ox_ref[rows, cols] = (o / l).astype(ox_ref.dtype)
        return carry

    lax.fori_loop(0, n_blk, body, 0)


def _swa_mem(qa, ka, va, qx, mk, mv, sink_row, *, qt):
    b, t, _ = qa.shape
    m = mk.shape[1]
    qtile = lambda c: pl.BlockSpec((None, qt, c), lambda i, j: (i, j, 0))
    whole = lambda r, c: pl.BlockSpec((None, r, c), lambda i, j: (i, 0, 0))
    return pl.pallas_call(
        functools.partial(_swa_mem_kernel, qt=qt),
        grid=(b, t // qt),
        in_specs=[qtile(Q_A), whole(t, KV_A), whole(t, KV_A), qtile(Q_X), whole(m, Q_X), whole(m, Q_X),
                  pl.BlockSpec((1, Q_A), lambda i, j: (0, 0)),
                  pl.BlockSpec((2, BLK, 2 * BLK), lambda i, j: (0, 0, 0))],
        out_specs=[qtile(Q_A), qtile(Q_X)],
        out_shape=[jax.ShapeDtypeStruct((b, t, Q_A), BF16), jax.ShapeDtypeStruct((b, t, Q_X), BF16)],
        compiler_params=_params(2),
        name="swa_mem",
    )(qa, ka, va, qx, mk, mv, sink_row, jnp.asarray(_band_bias()))


def _dil_kernel(q_ref, k_ref, v_ref, bias_ref, ob_ref, *acc, qt):
    j = pl.program_id(1)
    n_pairs = Q_B // LANES
    patterns = sorted(DIL_PATTERNS, key=lambda wr: -wr[1])
    for idx, (w, r) in enumerate(patterns):
        assert w // r == BLK and qt % (r * BLK) == 0
        n_blk = qt // (r * BLK)

        def body(step, carry, r=r, n_blk=n_blk, first=(idx == 0)):
            a = lax.div(step, n_blk)
            i = lax.rem(step, n_blk)
            n = j * n_blk + i

            def rows(blk):
                if r == 1:
                    return pl.ds(pl.multiple_of(blk * BLK, BLK), BLK)
                return pl.ds(blk * (BLK * r) + a, BLK, stride=r)

            prev = jnp.maximum(n - 1, 0)
            bias = bias_ref[jnp.minimum(n, 1)]
            for p in range(n_pairs):
                acc_o, acc_m, acc_l = acc[3 * p:3 * p + 3]
                q = q_ref[p, rows(i), :].astype(BF16)
                kk = jnp.concatenate([k_ref[p, rows(prev), :], k_ref[p, rows(n), :]], axis=0).astype(BF16)
                vv = jnp.concatenate([v_ref[p, rows(prev), :], v_ref[p, rows(n), :]], axis=0).astype(BF16)
                o, m, l = _pair_attend(q, kk, vv, bias)
                if not first:
                    m_old = acc_m[rows(i), :]
                    m_new = jnp.maximum(m_old, m)
                    e_old = jnp.exp(m_old - m_new)
                    e = jnp.exp(m - m_new)
                    o = acc_o[rows(i), :] * e_old + o * e
                    l = acc_l[rows(i), :] * e_old + l * e
                    m = m_new
                acc_o[rows(i), :] = o
                acc_m[rows(i), :] = m
                acc_l[rows(i), :] = l
            return carry

        lax.fori_loop(0, r * n_blk, body, 0, unroll=2)
    for p in range(n_pairs):
        ob_ref[:, p * LANES:(p + 1) * LANES] = (acc[3 * p][...] / acc[3 * p + 2][...]).astype(ob_ref.dtype)


def _dil(q, k, v, *, qt):
    b, n_pairs, t, _ = q.shape
    return pl.pallas_call(
        functools.partial(_dil_kernel, qt=qt),
        grid=(b, t // qt),
        in_specs=[pl.BlockSpec((None, n_pairs, qt, LANES), lambda i, j: (i, 0, j, 0)),
                  pl.BlockSpec((None, n_pairs, t, LANES), lambda i, j: (i, 0, 0, 0)),
                  pl.BlockSpec((None, n_pairs, t, LANES), lambda i, j: (i, 0, 0, 0)),
                  pl.BlockSpec((2, BLK, 2 * BLK), lambda i, j: (0, 0, 0))],
        out_specs=pl.BlockSpec((None, qt, n_pairs * LANES), lambda i, j: (i, j, 0)),
        out_shape=jax.ShapeDtypeStruct((b, t, n_pairs * LANES), BF16),
        scratch_shapes=[pltpu.VMEM((qt, LANES), F32)] * (3 * n_pairs),
        compiler_params=_params(2),
        name="dil",
    )(q, k, v, jnp.asarray(_band_bias()))


def _sample_kernel(qa_ref, kan_ref, van_ref, qb_ref, kbn_ref, vbn_ref, qx_ref,
                   cak_ref, cav_ref, cbk_ref, cbv_ref, cmk_ref, cmv_ref,
                   sink_ref, amask_ref, bcnt_ref,
                   oa_ref, ob_ref, ox_ref, sak_ref, sav_ref, sbk_ref, sbv_ref,
                   ka_s, va_s, kb_s, vb_s, *, s_len):
    def shift_and_stage(cache_ref, new_ref, out_ref, stage):
        c, lb = cache_ref.shape
        new = new_ref[...]
        new_t = jnp.concatenate([new, jnp.zeros((LANES - s_len, c), F32)], axis=0).T
        cache = cache_ref[...]
        stage[:, 0:lb] = cache.astype(BF16)
        stage[:, lb:] = new_t.astype(BF16)
        shifted = pltpu.roll(cache, lb - s_len, 1)
        if lb > LANES:
            out_ref[:, 0:lb - LANES] = shifted[:, 0:lb - LANES]
        keep = lax.broadcasted_iota(jnp.int32, (c, LANES), 1) < LANES - s_len
        out_ref[:, lb - LANES:] = jnp.where(keep, shifted[:, lb - LANES:], pltpu.roll(new_t, LANES - s_len, 1))

    shift_and_stage(cak_ref, kan_ref, sak_ref, ka_s)
    shift_and_stage(cav_ref, van_ref, sav_ref, va_s)
    shift_and_stage(cbk_ref, kbn_ref, sbk_ref, kb_s)
    shift_and_stage(cbv_ref, vbn_ref, sbv_ref, vb_s)

    def scores(qblk, keys_t):
        return jnp.dot(qblk.astype(BF16), keys_t, preferred_element_type=F32)

    def weighted(p, values_t):
        return lax.dot_general(p.astype(BF16), values_t, (((1,), (1,)), ((), ())), preferred_element_type=F32)

    def stack_heads(q, n_heads):
        return jnp.concatenate([q] * n_heads, axis=0)

    def own_head(shape):
        row_h = lax.broadcasted_iota(jnp.int32, shape, 0) // s_len
        col_h = lax.broadcasted_iota(jnp.int32, shape, 1) // HEAD_DIM
        return row_h == col_h

    def fold_heads(o_all, n_heads):
        o_all = jnp.where(own_head(o_all.shape), o_all, 0.0)
        o = o_all[0:s_len]
        for h in range(1, n_heads):
            o = o + o_all[h * s_len:(h + 1) * s_len]
        return o

    qa = qa_ref[...]
    lo = _lane_lo((s_len, LANES))
    pieces = []
    for p in range(Q_A // LANES):
        tile = qa[:, p * LANES:(p + 1) * LANES]
        pieces += [jnp.where(lo, tile, 0.0), jnp.where(lo, 0.0, tile)]
    s = scores(jnp.concatenate(pieces, axis=0), ka_s[...])
    s = jnp.where(amask_ref[...] > 0.0, s, -jnp.inf)
    m = jnp.max(s, axis=1, keepdims=True)
    p_ = jnp.exp(s - m)
    l = jnp.sum(p_, axis=1, keepdims=True)
    o = weighted(p_, va_s[...])
    sink = sink_ref[...]
    mm = jnp.maximum(m, sink)
    a = jnp.exp(m - mm)
    o = o * (a / (l * a + jnp.exp(sink - mm)))
    oa_ref[...] = jnp.concatenate(
        [jnp.where(lo, o[2 * p * s_len:(2 * p + 1) * s_len], o[(2 * p + 1) * s_len:(2 * p + 2) * s_len])
         for p in range(Q_A // LANES)], axis=1)

    qb = stack_heads(qb_ref[...], DIL_HEADS)
    s = scores(jnp.where(own_head(qb.shape), qb, 0.0), kb_s[...])
    cnt = bcnt_ref[...]
    s = jnp.where(cnt > 0.0, s, -jnp.inf)
    m = jnp.max(s, axis=1, keepdims=True)
    p_ = cnt * jnp.exp(s - m)
    l = jnp.sum(p_, axis=1, keepdims=True)
    ob_ref[...] = fold_heads(weighted(p_, vb_s[...]) / l, DIL_HEADS)

    qx = stack_heads(qx_ref[...], MEM_HEADS)
    s = scores(jnp.where(own_head(qx.shape), qx, 0.0), cmk_ref[...].astype(BF16))
    m = jnp.max(s, axis=1, keepdims=True)
    p_ = jnp.exp(s - m)
    l = jnp.sum(p_, axis=1, keepdims=True)
    ox_ref[...] = fold_heads(weighted(p_, cmv_ref[...].astype(BF16)) / l, MEM_HEADS)


def _sample_tables(s_len, lb_a, lb_b, na, nb):
    i = np.arange(s_len)[:, None]
    dist_a = lb_a + i - np.arange(na)[None, :]
    amask = ((dist_a >= 0) & (dist_a <= SWA_WINDOW)).astype(np.float32)
    dist_b = lb_b + i - np.arange(nb)[None, :]
    cnt = np.zeros(dist_b.shape, np.float32)
    for w, r in DIL_PATTERNS:
        cnt += ((dist_b >= 0) & (dist_b <= w) & (dist_b % r == 0)).astype(np.float32)
    return np.tile(amask, (SWA_KV_HEADS * SWA_GROUP, 1)), np.tile(cnt, (DIL_HEADS, 1))


def _sample_attend(qa, kan, van, qb, kbn, vbn, qx, cak, cav, cbk, cbv, cmk, cmv, sink_rows, *, s_len):
    nbat, _, lb_a = cak.shape
    lb_b = cbk.shape[2]
    mtok = cmk.shape[2]
    assert lb_a >= SWA_WINDOW and lb_b >= DIL_MAX_WINDOW, "every window position must lie inside the cache"
    assert lb_a % LANES == 0 and lb_b % LANES == 0 and s_len <= LANES
    na = lb_a + LANES
    nb = lb_b + LANES
    amask, bcnt = _sample_tables(s_len, lb_a, lb_b, na, nb)
    tok = lambda c: pl.BlockSpec((s_len, c), lambda n: (n, 0))
    per_b = lambda r, c: pl.BlockSpec((None, r, c), lambda n: (n, 0, 0))
    const = lambda a: pl.BlockSpec(a.shape, lambda n: (0, 0))
    n_tok = nbat * s_len
    return pl.pallas_call(
        functools.partial(_sample_kernel, s_len=s_len),
        grid=(nbat,),
        in_specs=[tok(Q_A), tok(KV_A), tok(KV_A), tok(Q_B), tok(KV_B), tok(KV_B), tok(Q_X),
                  per_b(KV_A, lb_a), per_b(KV_A, lb_a), per_b(KV_B, lb_b), per_b(KV_B, lb_b),
                  per_b(Q_X, mtok), per_b(Q_X, mtok),
                  const(sink_rows), const(amask), const(bcnt)],
        out_specs=[tok(Q_A), tok(Q_B), tok(Q_X),
                   per_b(KV_A, lb_a), per_b(KV_A, lb_a), per_b(KV_B, lb_b), per_b(KV_B, lb_b)],
        out_shape=[jax.ShapeDtypeStruct((n_tok, Q_A), F32), jax.ShapeDtypeStruct((n_tok, Q_B), F32),
                   jax.ShapeDtypeStruct((n_tok, Q_X), F32),
                   jax.ShapeDtypeStruct(cak.shape, F32), jax.ShapeDtypeStruct(cav.shape, F32),
                   jax.ShapeDtypeStruct(cbk.shape, F32), jax.ShapeDtypeStruct(cbv.shape, F32)],
        scratch_shapes=[pltpu.VMEM((KV_A, na), BF16), pltpu.VMEM((KV_A, na), BF16),
                        pltpu.VMEM((KV_B, nb), BF16), pltpu.VMEM((KV_B, nb), BF16)],
        compiler_params=_params(1),
        name="sample_attend",
    )(qa, kan, van, qb, kbn, vbn, qx, cak, cav, cbk, cbv, cmk, cmv,
      sink_rows, jnp.asarray(amask), jnp.asarray(bcnt))


def _finish_kernel(x_ref, oa_ref, ob_ref, ox_ref, wo_ref, g1_ref, g2_ref, wg_ref, wu_ref, wd_ref, g3_ref, out_ref,
                   *, ff_chunk):
    cat = jnp.concatenate([oa_ref[...].astype(BF16), ob_ref[...].astype(BF16), ox_ref[...].astype(BF16)], axis=1)
    x = x_ref[...]
    x = x + _rms(jnp.dot(cat, wo_ref[...], preferred_element_type=F32), g1_ref[...])
    h = _rms(x, g2_ref[...]).astype(BF16)
    d_ff = wg_ref.shape[1]
    f = 0.0
    for c0 in range(0, d_ff, ff_chunk):
        cols = slice(c0, c0 + ff_chunk)
        g = jnp.dot(h, wg_ref[:, cols], preferred_element_type=F32)
        u = jnp.dot(h, wu_ref[:, cols], preferred_element_type=F32)
        act = (g * (1.0 / (1.0 + jnp.exp(-g))) * u).astype(BF16)
        f = f + jnp.dot(act, wd_ref[cols, :], preferred_element_type=F32)
    out_ref[...] = x + _rms(f, g3_ref[...])


def _finish(x, oa, ob, ox, wo, g1, g2, wg, wu, wd, g3, *, tm, ff_chunk):
    n, d = x.shape
    row = lambda c: pl.BlockSpec((tm, c), lambda i: (i, 0))
    const = lambda a: pl.BlockSpec(a.shape, lambda i: (0, 0), pipeline_mode=pl.Buffered(1))
    return pl.pallas_call(
        functools.partial(_finish_kernel, ff_chunk=ff_chunk),
        grid=(n // tm,),
        in_specs=[row(d), row(Q_A), row(Q_B), row(Q_X)] + [const(a) for a in (wo, g1, g2, wg, wu, wd, g3)],
        out_specs=row(d),
        out_shape=jax.ShapeDtypeStruct((n, d), F32),
        compiler_params=_params(1),
        name="finish",
    )(x, oa, ob, ox, wo, g1, g2, wg, wu, wd, g3)


def _rope_tables(pos):
    half = HEAD_DIM // 2
    inv = ROPE_THETA ** (-jnp.arange(half, dtype=F32) / half)
    ang = pos.astype(F32)[:, None] * inv[None, :]
    cos = jnp.tile(jnp.cos(ang), (1, LANES // half))
    sin = jnp.tile(jnp.sin(ang), (1, LANES // half))
    first_half = (jnp.arange(LANES) % HEAD_DIM) < half
    return cos, jnp.where(first_half[None, :], -sin, sin)


def _swa_head_perm():
    return np.concatenate([np.arange(h * HEAD_DIM, (h + 1) * HEAD_DIM) for h in SWA_HEAD_ORDER])


def _seq_minor(c):
    b, n, h, dh = c.shape
    return jnp.transpose(c, (0, 2, 3, 1)).reshape(b, h * dh, n)


def _seq_major(a, heads):
    b, c, n = a.shape
    return jnp.transpose(a.reshape(b, heads, c // heads, n), (0, 3, 1, 2))


def kernel(x_prompt, x_sample, cache_swa_k, cache_swa_v, cache_dil_k, cache_dil_v, cache_mem_k, cache_mem_v,
           mem_prompt, g_pre_mix, w_in, sinks, w_mem_kv, w_o, g_post_mix, g_pre_ffn, w_gate, w_up, w_down,
           g_post_ffn):
    depth = w_in.shape[0]
    bp, t_p, d = x_prompt.shape
    bs, s_len, _ = x_sample.shape
    d_ff = w_gate.shape[2]
    perm = _swa_head_perm()
    cos_p, sin_p = _rope_tables(jnp.arange(t_p))
    cos_s, sin_s = _rope_tables(PAST_LEN + jnp.arange(s_len))
    tm_s = 256
    cos_s = jnp.tile(cos_s, (tm_s // s_len, 1))
    sin_s = jnp.tile(sin_s, (tm_s // s_len, 1))
    ff_chunk = d_ff // 2 if (d_ff // 2) % LANES == 0 else d_ff
    n_a = min(SWA_WINDOW, t_p)
    n_b = min(DIL_MAX_WINDOW, t_p)

    hp = x_prompt
    hs = x_sample.reshape(1, bs * s_len, d)
    outs = [[] for _ in range(10)]
    for l in range(depth):
        w_in_l = w_in[l]
        w_in_bf = jnp.concatenate([w_in_l[:, :Q_A][:, perm], w_in_l[:, Q_A:]], axis=1).astype(BF16)
        w_o_bf = jnp.concatenate([w_o[l][:Q_A][perm], w_o[l][Q_A:]], axis=0).astype(BF16)
        w_mem_bf = w_mem_kv[l].astype(BF16)
        wg, wu, wd = w_gate[l].astype(BF16), w_up[l].astype(BF16), w_down[l].astype(BF16)
        g0, g1, g2, g3 = (g[l][None, :] for g in (g_pre_mix, g_post_mix, g_pre_ffn, g_post_ffn))
        sink_heads = sinks[l].reshape(-1)[np.array(SWA_HEAD_ORDER)].astype(F32)
        sink_row = jnp.repeat(sink_heads, HEAD_DIM)[None, :]
        sink_rows = jnp.repeat(sink_heads, s_len)[:, None]

        qa, qb, qx, kah, vah, kbf, vbf, kat, vat, kbt, vbt = _proj(
            hp, g0, w_in_bf, cos_p, sin_p, tm=512, prompt=True, n_a=n_a, n_b=n_b)
        mkt, mvt, mkh, mvh = _memkv(mem_prompt, w_mem_bf)
        oa, ox = _swa_mem(qa, kah, vah, qx, mkh, mvh, sink_row, qt=1024)
        ob = _dil(qb, kbf, vbf, qt=2048)
        flat = lambda a: a.reshape(bp * t_p, a.shape[-1])
        hp = _finish(flat(hp), flat(oa), flat(ob), flat(ox), w_o_bf, g1, g2, wg, wu, wd, g3,
                     tm=512, ff_chunk=ff_chunk).reshape(bp, t_p, d)
        outs[0].append(_seq_major(kat, SWA_KV_HEADS))
        outs[1].append(_seq_major(vat, SWA_KV_HEADS))
        outs[2].append(_seq_major(kbt, DIL_HEADS))
        outs[3].append(_seq_major(vbt, DIL_HEADS))
        outs[4].append(_seq_major(mkt, MEM_HEADS))
        outs[5].append(_seq_major(mvt, MEM_HEADS))

        qa, qb, qx, kan, van, kbn, vbn = (a[0] for a in _proj(
            hs, g0, w_in_bf, cos_s, sin_s, tm=tm_s, prompt=False))
        oa, ob, ox, sak, sav, sbk, sbv = _sample_attend(
            qa, kan, van, qb, kbn, vbn, qx,
            _seq_minor(cache_swa_k[l]), _seq_minor(cache_swa_v[l]),
            _seq_minor(cache_dil_k[l]), _seq_minor(cache_dil_v[l]),
            _seq_minor(cache_mem_k[l]), _seq_minor(cache_mem_v[l]),
            sink_rows, s_len=s_len)
        hs = _finish(hs[0], oa, ob, ox, w_o_bf, g1, g2, wg, wu, wd, g3,
                     tm=256, ff_chunk=ff_chunk)[None]
        outs[6].append(_seq_major(sak, SWA_KV_HEADS))
        outs[7].append(_seq_major(sav, SWA_KV_HEADS))
        outs[8].append(_seq_major(sbk, DIL_HEADS))
        outs[9].append(_seq_major(sbv, DIL_HEADS))

    return (hp, hs.reshape(bs, s_len, d)) + tuple(jnp.stack(o) for o in outs)
```

```python
import functools

import numpy as np
import jax
import jax.numpy as jnp
from jax import lax
from jax.experimental import pallas as pl
from jax.experimental.pallas import tpu as pltpu

F32 = jnp.float32
BF16 = jnp.bfloat16

HEAD_DIM = 64
SWA_KV_HEADS = 2
SWA_GROUP = 3
SWA_WINDOW = 128
DIL_HEADS = 6
DIL_PATTERNS = ((128, 1), (512, 4), (2048, 16))
DIL_MAX_WINDOW = 2048
MEM_HEADS = 4
BLK = 128
ROPE_THETA = 10000.0
EPS = 1e-6
SCALE = HEAD_DIM ** -0.5
PAST_LEN = 16384

Q_A = SWA_KV_HEADS * SWA_GROUP * HEAD_DIM
KV_A = SWA_KV_HEADS * HEAD_DIM
Q_B = DIL_HEADS * HEAD_DIM
KV_B = DIL_HEADS * HEAD_DIM
Q_X = MEM_HEADS * HEAD_DIM
IN_SIZES = (Q_A, KV_A, KV_A, Q_B, KV_B, KV_B, Q_X)
IN_OFFS = tuple(int(v) for v in np.cumsum((0,) + IN_SIZES))

LANES = 128
VMEM_LIMIT = 56 * 1024 * 1024

SWA_HEAD_ORDER = (0, 3, 1, 4, 2, 5)


def _params(n_axes):
    return pltpu.CompilerParams(dimension_semantics=("arbitrary",) * n_axes,
                                vmem_limit_bytes=VMEM_LIMIT)


def _rms(x, g):
    return x * lax.rsqrt(jnp.mean(x * x, axis=-1, keepdims=True) + EPS) * g


def _lane_lo(shape):
    return lax.broadcasted_iota(jnp.int32, shape, len(shape) - 1) % LANES < HEAD_DIM


def _proj_kernel(x_ref, g_ref, w_ref, cos_ref, sin_ref, *out_refs, prompt, n_a, n_b):
    tm = x_ref.shape[0]
    u = _rms(x_ref[...], g_ref[...]).astype(BF16)
    cos = cos_ref[...]
    sin = sin_ref[...]
    first_half = lax.broadcasted_iota(jnp.int32, cos.shape, 1) % HEAD_DIM < HEAD_DIM // 2

    def proj(i):
        return jnp.dot(u, w_ref[:, IN_OFFS[i]:IN_OFFS[i + 1]], preferred_element_type=F32)

    def rope(z):
        tiles = []
        for s in range(z.shape[1] // LANES):
            zs = z[:, s * LANES:(s + 1) * LANES]
            partner = jnp.where(first_half, pltpu.roll(zs, LANES - HEAD_DIM // 2, 1),
                                pltpu.roll(zs, HEAD_DIM // 2, 1))
            tiles.append(zs * cos + partner * sin)
        return jnp.concatenate(tiles, axis=1) if len(tiles) > 1 else tiles[0]

    qa = rope(proj(0))
    ka = rope(proj(1))
    va = proj(2)
    qb = rope(proj(3))
    kb = rope(proj(4))
    vb = proj(5)
    qx = proj(6)
    qa_ref, qb_ref, qx_ref, ka_ref, va_ref, kb_ref, vb_ref = out_refs[:7]
    qa_ref[...] = (qa * SCALE).astype(qa_ref.dtype)
    qx_ref[...] = (qx * SCALE).astype(qx_ref.dtype)
    ka_ref[...] = ka.astype(ka_ref.dtype)
    va_ref[...] = va.astype(va_ref.dtype)
    if not prompt:
        qb_ref[...] = qb * SCALE
        kb_ref[...] = kb
        vb_ref[...] = vb
    else:
        for p in range(Q_B // LANES):
            cols = slice(p * LANES, (p + 1) * LANES)
            qb_ref[p] = qb[:, cols] * SCALE
            kb_ref[p] = kb[:, cols]
            vb_ref[p] = vb[:, cols]
        kat_ref, vat_ref, kbt_ref, vbt_ref = out_refs[7:]
        j = pl.program_id(1)
        nj = pl.num_programs(1)

        @pl.when(j >= nj - n_b // tm)
        def _():
            kbt_ref[...] = kb.T
            vbt_ref[...] = vb.T

        @pl.when(j == nj - 1)
        def _():
            kat_ref[...] = ka[tm - n_a:, :].T
            vat_ref[...] = va[tm - n_a:, :].T


def _proj(x, g, w_bf, cos, sin, *, tm, prompt, n_a=0, n_b=0):
    b, t, d = x.shape
    nj = t // tm
    n_rope = cos.shape[0] // tm
    tile = lambda c: pl.BlockSpec((None, tm, c), lambda i, j: (i, j, 0))
    rope = pl.BlockSpec((tm, LANES), lambda i, j: (j % n_rope, 0))
    dt = BF16 if prompt else F32
    widths = (Q_A, Q_B, Q_X, KV_A, KV_A, KV_B, KV_B)
    out_specs = [tile(c) for c in widths]
    out_shape = [jax.ShapeDtypeStruct((b, t, c), dt) for c in widths]
    if prompt:
        assert n_a <= tm and n_a % LANES == 0 and n_b % tm == 0 and n_b <= t
        n_pairs = Q_B // LANES
        for i in (1, 5, 6):
            out_specs[i] = pl.BlockSpec((None, n_pairs, tm, LANES), lambda i, j: (i, 0, j, 0))
            out_shape[i] = jax.ShapeDtypeStruct((b, n_pairs, t, LANES), F32)
        first_b = nj - n_b // tm
        out_specs += [pl.BlockSpec((None, KV_A, n_a), lambda i, j: (i, 0, 0))] * 2
        out_specs += [pl.BlockSpec((None, KV_B, tm), lambda i, j: (i, 0, jnp.maximum(j - first_b, 0)))] * 2
        out_shape += [jax.ShapeDtypeStruct((b, KV_A, n_a), F32)] * 2 + [jax.ShapeDtypeStruct((b, KV_B, n_b), F32)] * 2
    return pl.pallas_call(
        functools.partial(_proj_kernel, prompt=prompt, n_a=n_a, n_b=n_b),
        grid=(b, nj),
        in_specs=[tile(d), pl.BlockSpec((1, d), lambda i, j: (0, 0)),
                  pl.BlockSpec(w_bf.shape, lambda i, j: (0, 0), pipeline_mode=pl.Buffered(1)),
                  rope, rope],
        out_specs=out_specs,
        out_shape=out_shape,
        compiler_params=_params(2),
        name="proj_prompt" if prompt else "proj_sample",
    )(x, g, w_bf, cos, sin)


def _memkv_kernel(mem_ref, w_ref, mkt_ref, mvt_ref, mkh_ref, mvh_ref):
    z = jnp.dot(mem_ref[...].astype(BF16), w_ref[...], preferred_element_type=F32)
    mk = z[:, :Q_X]
    mv = z[:, Q_X:]
    mkt_ref[...] = mk.T
    mvt_ref[...] = mv.T
    mkh_ref[...] = mk.astype(BF16)
    mvh_ref[...] = mv.astype(BF16)


def _memkv(mem, w_bf):
    b, m, d = mem.shape
    blk = lambda r, c: pl.BlockSpec((None, r, c), lambda i: (i, 0, 0))
    return pl.pallas_call(
        _memkv_kernel,
        grid=(b,),
        in_specs=[blk(m, d), pl.BlockSpec(w_bf.shape, lambda i: (0, 0))],
        out_specs=[blk(Q_X, m)] * 2 + [blk(m, Q_X)] * 2,
        out_shape=[jax.ShapeDtypeStruct((b, Q_X, m), F32)] * 2 + [jax.ShapeDtypeStruct((b, m, Q_X), BF16)] * 2,
        compiler_params=_params(1),
        name="memkv",
    )(mem, w_bf)


def _pair_attend(q, kk, vv, bias):
    tq = q.shape[0]
    zero = jnp.zeros_like(q)
    q_lo = _lane_lo(q.shape)
    q2 = jnp.concatenate([jnp.where(q_lo, q, zero), jnp.where(q_lo, zero, q)], axis=0)
    s = lax.dot_general(q2, kk, (((1,), (1,)), ((), ())), preferred_element_type=F32)
    stats = []
    for h in range(2):
        sh = s[h * tq:(h + 1) * tq]
        if bias is not None:
            sh = sh + bias
        m = jnp.max(sh, axis=1, keepdims=True)
        p = jnp.exp(sh - m)
        stats.append((m, jnp.sum(p, axis=1, keepdims=True), p.astype(BF16)))
    v_lo = _lane_lo(vv.shape)
    vzero = jnp.zeros_like(vv)
    o = (jnp.dot(stats[0][2], jnp.where(v_lo, vv, vzero), preferred_element_type=F32)
         + jnp.dot(stats[1][2], jnp.where(v_lo, vzero, vv), preferred_element_type=F32))
    o_lo = _lane_lo(o.shape)
    m = jnp.where(o_lo, stats[0][0], stats[1][0])
    l = jnp.where(o_lo, stats[0][1], stats[1][1])
    return o, m, l


def _band_bias():
    qi = np.arange(BLK)[:, None]
    c = np.arange(2 * BLK)[None, :]
    in_prev = (c < BLK) & (c >= qi)
    in_cur = (c >= BLK) & (c - BLK <= qi)
    allowed = np.stack([in_cur, in_prev | in_cur])
    return np.where(allowed, 0.0, -np.inf).astype(np.float32)


def _band_rows(j, i, n_blk):
    n = j * n_blk + i
    r0 = pl.multiple_of(i * BLK, BLK)
    k_cur = pl.multiple_of(n * BLK, BLK)
    k_prev = pl.multiple_of(jnp.maximum(n - 1, 0) * BLK, BLK)
    return n, r0, k_prev, k_cur


def _swa_mem_kernel(qa_ref, ka_ref, va_ref, qx_ref, mk_ref, mv_ref, sink_ref, bias_ref, oa_ref, ox_ref, *, qt):
    j = pl.program_id(1)
    n_blk = qt // BLK

    def body(i, carry):
        n, r0, k_prev, k_cur = _band_rows(j, i, n_blk)
        rows = pl.ds(r0, BLK)
        mask = bias_ref[jnp.minimum(n, 1)]
        kk = jnp.concatenate([ka_ref[pl.ds(k_prev, BLK), :], ka_ref[pl.ds(k_cur, BLK), :]], axis=0)
        vv = jnp.concatenate([va_ref[pl.ds(k_prev, BLK), :], va_ref[pl.ds(k_cur, BLK), :]], axis=0)
        for p in range(Q_A // LANES):
            cols = slice(p * LANES, (p + 1) * LANES)
            o, m, l = _pair_attend(qa_ref[rows, cols], kk, vv, mask)
            sink = sink_ref[:, cols]
            mm = jnp.maximum(m, sink)
            a = jnp.exp(m - mm)
            den = l * a + jnp.exp(sink - mm)
            oa_ref[rows, cols] = (o * (a / den)).astype(oa_ref.dtype)
        for p in range(Q_X // LANES):
            cols = slice(p * LANES, (p + 1) * LANES)
            o, m, l = _pair_attend(qx_ref[rows, cols], mk_ref[:, cols], mv_ref[:, cols], None)
            ox_ref[rows, cols] = (o / l).astype(ox_ref.dtype)
        return carry

    lax.fori_loop(0, n_blk, body, 0)


def _swa_mem(qa, ka, va, qx, mk, mv, sink_row, *, qt):
    b, t, _ = qa.shape
    m = mk.shape[1]
    qtile = lambda c: pl.BlockSpec((None, qt, c), lambda i, j: (i, j, 0))
    whole = lambda r, c: pl.BlockSpec((None, r, c), lambda i, j: (i, 0, 0))
    return pl.pallas_call(
        functools.partial(_swa_mem_kernel, qt=qt),
        grid=(b, t // qt),
        in_specs=[qtile(Q_A), whole(t, KV_A), whole(t, KV_A), qtile(Q_X), whole(m, Q_X), whole(m, Q_X),
                  pl.BlockSpec((1, Q_A), lambda i, j: (0, 0)),
                  pl.BlockSpec((2, BLK, 2 * BLK), lambda i, j: (0, 0, 0))],
        out_specs=[qtile(Q_A), qtile(Q_X)],
        out_shape=[jax.ShapeDtypeStruct((b, t, Q_A), BF16), jax.ShapeDtypeStruct((b, t, Q_X), BF16)],
        compiler_params=_params(2),
        name="swa_mem",
    )(qa, ka, va, qx, mk, mv, sink_row, jnp.asarray(_band_bias()))


def _dil_kernel(q_ref, k_ref, v_ref, bias_ref, ob_ref, *acc, qt):
    j = pl.program_id(1)
    n_pairs = Q_B // LANES
    patterns = sorted(DIL_PATTERNS, key=lambda wr: -wr[1])
    for idx, (w, r) in enumerate(patterns):
        assert w // r == BLK and qt % (r * BLK) == 0
        n_blk = qt // (r * BLK)

        def body(step, carry, r=r, n_blk=n_blk, first=(idx == 0)):
            a = lax.div(step, n_blk)
            i = lax.rem(step, n_blk)
            n = j * n_blk + i

            def rows(blk):
                if r == 1:
                    return pl.ds(pl.multiple_of(blk * BLK, BLK), BLK)
                return pl.ds(blk * (BLK * r) + a, BLK, stride=r)

            prev = jnp.maximum(n - 1, 0)
            bias = bias_ref[jnp.minimum(n, 1)]
            for p in range(n_pairs):
                acc_o, acc_m, acc_l = acc[3 * p:3 * p + 3]
                q = q_ref[p, rows(i), :].astype(BF16)
                kk = jnp.concatenate([k_ref[p, rows(prev), :], k_ref[p, rows(n), :]], axis=0).astype(BF16)
                vv = jnp.concatenate([v_ref[p, rows(prev), :], v_ref[p, rows(n), :]], axis=0).astype(BF16)
                o, m, l = _pair_attend(q, kk, vv, bias)
                if not first:
                    m_old = acc_m[rows(i), :]
                    m_new = jnp.maximum(m_old, m)
                    e_old = jnp.exp(m_old - m_new)
                    e = jnp.exp(m - m_new)
                    o = acc_o[rows(i), :] * e_old + o * e
                    l = acc_l[rows(i), :] * e_old + l * e
                    m = m_new
                acc_o[rows(i), :] = o
                acc_m[rows(i), :] = m
                acc_l[rows(i), :] = l
            return carry

        lax.fori_loop(0, r * n_blk, body, 0, unroll=2)
    for p in range(n_pairs):
        ob_ref[:, p * LANES:(p + 1) * LANES] = (acc[3 * p][...] / acc[3 * p + 2][...]).astype(ob_ref.dtype)


def _dil(q, k, v, *, qt):
    b, n_pairs, t, _ = q.shape
    return pl.pallas_call(
        functools.partial(_dil_kernel, qt=qt),
        grid=(b, t // qt),
        in_specs=[pl.BlockSpec((None, n_pairs, qt, LANES), lambda i, j: (i, 0, j, 0)),
                  pl.BlockSpec((None, n_pairs, t, LANES), lambda i, j: (i, 0, 0, 0)),
                  pl.BlockSpec((None, n_pairs, t, LANES), lambda i, j: (i, 0, 0, 0)),
                  pl.BlockSpec((2, BLK, 2 * BLK), lambda i, j: (0, 0, 0))],
        out_specs=pl.BlockSpec((None, qt, n_pairs * LANES), lambda i, j: (i, j, 0)),
        out_shape=jax.ShapeDtypeStruct((b, t, n_pairs * LANES), BF16),
        scratch_shapes=[pltpu.VMEM((qt, LANES), F32)] * (3 * n_pairs),
        compiler_params=_params(2),
        name="dil",
    )(q, k, v, jnp.asarray(_band_bias()))


def _shift_and_stage(cache_ref, new_ref, out_ref, stage, s_len):
    c, lb = cache_ref.shape
    new = new_ref[...]
    new_t = jnp.concatenate([new, jnp.zeros((LANES - s_len, c), F32)], axis=0).T
    cache = cache_ref[...]
    stage[:, 0:lb] = cache.astype(BF16)
    stage[:, lb:] = new_t.astype(BF16)
    shifted = pltpu.roll(cache, lb - s_len, 1)
    if lb > LANES:
        out_ref[:, 0:lb - LANES] = shifted[:, 0:lb - LANES]
    keep = lax.broadcasted_iota(jnp.int32, (c, LANES), 1) < LANES - s_len
    out_ref[:, lb - LANES:] = jnp.where(keep, shifted[:, lb - LANES:], pltpu.roll(new_t, LANES - s_len, 1))


def _scores_t(qblk, keys_t):
    return jnp.dot(qblk.astype(BF16), keys_t, preferred_element_type=F32)


def _weighted_t(p, values_t):
    return lax.dot_general(p.astype(BF16), values_t, (((1,), (1,)), ((), ())), preferred_element_type=F32)


def _sample_small_kernel(qa_ref, kan_ref, van_ref, qx_ref, cak_ref, cav_ref, cmk_ref, cmv_ref, sink_ref, amask_ref,
                         oa_ref, ox_ref, sak_ref, sav_ref, ka_s, va_s, *, s_len):
    _shift_and_stage(cak_ref, kan_ref, sak_ref, ka_s, s_len)
    _shift_and_stage(cav_ref, van_ref, sav_ref, va_s, s_len)

    qa = qa_ref[...]
    lo = _lane_lo((s_len, LANES))
    pieces = []
    for p in range(Q_A // LANES):
        tile = qa[:, p * LANES:(p + 1) * LANES]
        pieces += [jnp.where(lo, tile, 0.0), jnp.where(lo, 0.0, tile)]
    s = _scores_t(jnp.concatenate(pieces, axis=0), ka_s[...])
    s = jnp.where(amask_ref[...] > 0.0, s, -jnp.inf)
    m = jnp.max(s, axis=1, keepdims=True)
    p_ = jnp.exp(s - m)
    l = jnp.sum(p_, axis=1, keepdims=True)
    o = _weighted_t(p_, va_s[...])
    sink = sink_ref[...]
    mm = jnp.maximum(m, sink)
    a = jnp.exp(m - mm)
    o = o * (a / (l * a + jnp.exp(sink - mm)))
    oa_ref[...] = jnp.concatenate(
        [jnp.where(lo, o[2 * p * s_len:(2 * p + 1) * s_len], o[(2 * p + 1) * s_len:(2 * p + 2) * s_len])
         for p in range(Q_A // LANES)], axis=1)

    qx = jnp.concatenate([qx_ref[...]] * MEM_HEADS, axis=0)
    row_h = lax.broadcasted_iota(jnp.int32, qx.shape, 0) // s_len
    col_h = lax.broadcasted_iota(jnp.int32, qx.shape, 1) // HEAD_DIM
    own = row_h == col_h
    s = _scores_t(jnp.where(own, qx, 0.0), cmk_ref[...].astype(BF16))
    m = jnp.max(s, axis=1, keepdims=True)
    p_ = jnp.exp(s - m)
    l = jnp.sum(p_, axis=1, keepdims=True)
    o_all = jnp.where(own, _weighted_t(p_, cmv_ref[...].astype(BF16)) / l, 0.0)
    o = o_all[0:s_len]
    for h in range(1, MEM_HEADS):
        o = o + o_all[h * s_len:(h + 1) * s_len]
    ox_ref[...] = o


def _sample_tables(s_len, lb_a, lb_b, na, nb):
    i = np.arange(s_len)[:, None]
    dist_a = lb_a + i - np.arange(na)[None, :]
    amask = ((dist_a >= 0) & (dist_a <= SWA_WINDOW)).astype(np.float32)
    dist_b = lb_b + i - np.arange(nb)[None, :]
    cnt = np.zeros(dist_b.shape, np.float32)
    for w, r in DIL_PATTERNS:
        cnt += ((dist_b >= 0) & (dist_b <= w) & (dist_b % r == 0)).astype(np.float32)
    return np.tile(amask, (SWA_KV_HEADS * SWA_GROUP, 1)), np.tile(cnt, (2, 1))


def _sample_small(qa, kan, van, qx, cak, cav, cmk, cmv, sink_rows, amask, *, s_len):
    nbat, _, lb_a = cak.shape
    mtok = cmk.shape[2]
    na = amask.shape[1]
    tok = lambda c: pl.BlockSpec((s_len, c), lambda n: (n, 0))
    per_b = lambda r, c: pl.BlockSpec((None, r, c), lambda n: (n, 0, 0))
    const = lambda a: pl.BlockSpec(a.shape, lambda n: (0, 0))
    n_tok = nbat * s_len
    return pl.pallas_call(
        functools.partial(_sample_small_kernel, s_len=s_len),
        grid=(nbat,),
        in_specs=[tok(Q_A), tok(KV_A), tok(KV_A), tok(Q_X),
                  per_b(KV_A, lb_a), per_b(KV_A, lb_a), per_b(Q_X, mtok), per_b(Q_X, mtok),
                  const(sink_rows), const(amask)],
        out_specs=[tok(Q_A), tok(Q_X), per_b(KV_A, lb_a), per_b(KV_A, lb_a)],
        out_shape=[jax.ShapeDtypeStruct((n_tok, Q_A), F32), jax.ShapeDtypeStruct((n_tok, Q_X), F32),
                   jax.ShapeDtypeStruct(cak.shape, F32), jax.ShapeDtypeStruct(cav.shape, F32)],
        scratch_shapes=[pltpu.VMEM((KV_A, na), BF16), pltpu.VMEM((KV_A, na), BF16)],
        compiler_params=_params(1),
        name="sample_small",
    )(qa, kan, van, qx, cak, cav, cmk, cmv, sink_rows, amask)


def _dil_unit(qn_ref, kn_ref, vn_ref, ck_ref, cv_ref, cnt_ref, obs_ref, sk_ref, sv_ref, k_s, v_s, s_len):
    _shift_and_stage(ck_ref, kn_ref, sk_ref, k_s, s_len)
    _shift_and_stage(cv_ref, vn_ref, sv_ref, v_s, s_len)
    qn = qn_ref[...]
    lo = _lane_lo(qn.shape)
    s = _scores_t(jnp.concatenate([jnp.where(lo, qn, 0.0), jnp.where(lo, 0.0, qn)], axis=0), k_s[...])
    cnt = cnt_ref[...]
    s = jnp.where(cnt > 0.0, s, -jnp.inf)
    m = jnp.max(s, axis=1, keepdims=True)
    p_ = cnt * jnp.exp(s - m)
    l = jnp.sum(p_, axis=1, keepdims=True)
    o = _weighted_t(p_, v_s[...]) / l
    obs_ref[...] = jnp.where(lo, o[0:s_len], o[s_len:2 * s_len])


def _mix_out(x_ref, oa_ref, ob_ref, ox_ref, wo_ref, g1_ref, g2_ref):
    cat = jnp.concatenate([oa_ref[...].astype(BF16), ob_ref[...].astype(BF16), ox_ref[...].astype(BF16)], axis=1)
    x = x_ref[...] + _rms(jnp.dot(cat, wo_ref[...], preferred_element_type=F32), g1_ref[...])
    return x, _rms(x, g2_ref[...]).astype(BF16)


def _gate_up(h, wg, wu):
    g = jnp.dot(h, wg, preferred_element_type=F32)
    u = jnp.dot(h, wu, preferred_element_type=F32)
    return (g * (1.0 / (1.0 + jnp.exp(-g))) * u).astype(BF16)


def _ffn_chunk(h, wg, wu, wd):
    return jnp.dot(_gate_up(h, wg, wu), wd, preferred_element_type=F32)


def _finish_kernel(x_ref, oa_ref, ob_ref, ox_ref, wo_ref, g1_ref, g2_ref, wg_ref, wu_ref, wd_ref, g3_ref, out_ref,
                   *, ff_chunk):
    x, h = _mix_out(x_ref, oa_ref, ob_ref, ox_ref, wo_ref, g1_ref, g2_ref)
    f = 0.0
    for c0 in range(0, wg_ref.shape[1], ff_chunk):
        cols = slice(c0, c0 + ff_chunk)
        f = f + _ffn_chunk(h, wg_ref[:, cols], wu_ref[:, cols], wd_ref[cols, :])
    out_ref[...] = x + _rms(f, g3_ref[...])


def _finish(x, oa, ob, ox, wo, g1, g2, wg, wu, wd, g3, *, tm, ff_chunk):
    n, d = x.shape
    row = lambda c: pl.BlockSpec((tm, c), lambda i: (i, 0))
    const = lambda a: pl.BlockSpec(a.shape, lambda i: (0,) * a.ndim, pipeline_mode=pl.Buffered(1))
    return pl.pallas_call(
        functools.partial(_finish_kernel, ff_chunk=ff_chunk),
        grid=(n // tm,),
        in_specs=[row(d), row(Q_A), row(Q_B), row(Q_X)] + [const(a) for a in (wo, g1, g2, wg, wu, wd, g3)],
        out_specs=row(d),
        out_shape=jax.ShapeDtypeStruct((n, d), F32),
        compiler_params=_params(1),
        name="finish",
    )(x, oa, ob, ox, wo, g1, g2, wg, wu, wd, g3)


def _finish_shift_kernel(x_ref, oa_ref, ob_ref, ox_ref, wo_ref, g1_ref, g2_ref, wg_ref, wu_ref, wd_ref, g3_ref,
                         qn_ref, kn_ref, vn_ref, ck_ref, cv_ref, cnt_ref,
                         out_ref, obs_ref, sk_ref, sv_ref,
                         x1_s, h_s, f_s, act_s, k_s, v_s, *, s_len, ups, ff_chunk):
    c = pl.program_id(1)
    last = pl.num_programs(1) - 1
    d_ff = wg_ref.shape[1]

    def units():
        for k in range(ups):
            _dil_unit(qn_ref.at[k], kn_ref.at[k], vn_ref.at[k], ck_ref.at[k], cv_ref.at[k], cnt_ref,
                      obs_ref.at[k], sk_ref.at[k], sv_ref.at[k], k_s, v_s, s_len)

    @pl.when(c == 0)
    def _():
        units()
        x, h = _mix_out(x_ref, oa_ref, ob_ref, ox_ref, wo_ref, g1_ref, g2_ref)
        x1_s[...] = x
        h_s[...] = h
        f_s[...] = jnp.zeros_like(f_s)
        act_s[...] = _gate_up(h, wg_ref[:, 0:ff_chunk], wu_ref[:, 0:ff_chunk])

    @pl.when((c > 0) & (c < last))
    def _():
        units()
        prev = pl.ds(pl.multiple_of((c - 1) * ff_chunk, ff_chunk), ff_chunk)
        cur = pl.ds(pl.multiple_of(c * ff_chunk, ff_chunk), ff_chunk)
        down = jnp.dot(act_s[...], wd_ref[prev, :], preferred_element_type=F32)
        act_s[...] = _gate_up(h_s[...], wg_ref[:, cur], wu_ref[:, cur])
        f_s[...] += down

    @pl.when(c == last)
    def _():
        units()
        f = f_s[...] + jnp.dot(act_s[...], wd_ref[d_ff - ff_chunk:d_ff, :], preferred_element_type=F32)
        out_ref[...] = x1_s[...] + _rms(f, g3_ref[...])


def _finish_shift(x, oa, ob, ox, wo, g1, g2, wg, wu, wd, g3, qn, kn, vn, ck, cv, cnt, *, tm, s_len, ff_chunk):
    n, d = x.shape
    n_sub = wg.shape[1] // ff_chunk + 1
    n_units, cw, lb = ck.shape
    steps = (n // tm) * n_sub
    assert n_units % steps == 0, "cache units must spread evenly over the grid steps"
    ups = n_units // steps
    row = lambda c: pl.BlockSpec((tm, c), lambda i, c_: (i, 0))
    const = lambda a: pl.BlockSpec(a.shape, lambda i, c_: (0,) * a.ndim, pipeline_mode=pl.Buffered(1))
    unit = lambda r, c: pl.BlockSpec((ups, r, c), lambda i, c_: (i * n_sub + c_, 0, 0))
    return pl.pallas_call(
        functools.partial(_finish_shift_kernel, s_len=s_len, ups=ups, ff_chunk=ff_chunk),
        grid=(n // tm, n_sub),
        in_specs=[row(d), row(Q_A), row(Q_B), row(Q_X)] + [const(a) for a in (wo, g1, g2, wg, wu, wd, g3)]
                 + [unit(s_len, cw)] * 3 + [unit(cw, lb)] * 2 + [const(cnt)],
        out_specs=[row(d), unit(s_len, cw), unit(cw, lb), unit(cw, lb)],
        out_shape=[jax.ShapeDtypeStruct((n, d), F32), jax.ShapeDtypeStruct(qn.shape, F32),
                   jax.ShapeDtypeStruct(ck.shape, F32), jax.ShapeDtypeStruct(cv.shape, F32)],
        scratch_shapes=[pltpu.VMEM((tm, d), F32), pltpu.VMEM((tm, d), BF16), pltpu.VMEM((tm, d), F32),
                        pltpu.VMEM((tm, ff_chunk), BF16),
                        pltpu.VMEM((cw, lb + LANES), BF16), pltpu.VMEM((cw, lb + LANES), BF16)],
        compiler_params=_params(2),
        name="finish_shift",
    )(x, oa, ob, ox, wo, g1, g2, wg, wu, wd, g3, qn, kn, vn, ck, cv, cnt)


def _rope_tables(pos):
    half = HEAD_DIM // 2
    inv = ROPE_THETA ** (-jnp.arange(half, dtype=F32) / half)
    ang = pos.astype(F32)[:, None] * inv[None, :]
    cos = jnp.tile(jnp.cos(ang), (1, LANES // half))
    sin = jnp.tile(jnp.sin(ang), (1, LANES // half))
    first_half = (jnp.arange(LANES) % HEAD_DIM) < half
    return cos, jnp.where(first_half[None, :], -sin, sin)


def _swa_head_perm():
    return np.concatenate([np.arange(h * HEAD_DIM, (h + 1) * HEAD_DIM) for h in SWA_HEAD_ORDER])


def _seq_minor(c):
    b, n, h, dh = c.shape
    return jnp.transpose(c, (0, 2, 3, 1)).reshape(b, h * dh, n)


def _seq_major(a, heads):
    b, c, n = a.shape
    return jnp.transpose(a.reshape(b, heads, c // heads, n), (0, 3, 1, 2))


def kernel(x_prompt, x_sample, cache_swa_k, cache_swa_v, cache_dil_k, cache_dil_v, cache_mem_k, cache_mem_v,
           mem_prompt, g_pre_mix, w_in, sinks, w_mem_kv, w_o, g_post_mix, g_pre_ffn, w_gate, w_up, w_down,
           g_post_ffn):
    depth = w_in.shape[0]
    bp, t_p, d = x_prompt.shape
    bs, s_len, _ = x_sample.shape
    d_ff = w_gate.shape[2]
    perm = _swa_head_perm()
    cos_p, sin_p = _rope_tables(jnp.arange(t_p))
    cos_s, sin_s = _rope_tables(PAST_LEN + jnp.arange(s_len))
    tm_s = 256
    cos_s = jnp.tile(cos_s, (tm_s // s_len, 1))
    sin_s = jnp.tile(sin_s, (tm_s // s_len, 1))
    ff_chunk = 2 * LANES
    assert d_ff % ff_chunk == 0
    n_pairs = Q_B // LANES
    lb_a = cache_swa_k.shape[2]
    lb_b = cache_dil_k.shape[2]
    assert lb_a >= SWA_WINDOW and lb_b >= DIL_MAX_WINDOW, "every window position must lie inside the cache"
    assert lb_a % LANES == 0 and lb_b % LANES == 0 and s_len <= LANES
    amask, bcnt = (jnp.asarray(a) for a in _sample_tables(s_len, lb_a, lb_b, lb_a + LANES, lb_b + LANES))
    n_a = min(SWA_WINDOW, t_p)
    n_b = min(DIL_MAX_WINDOW, t_p)

    hp = x_prompt
    hs = x_sample.reshape(1, bs * s_len, d)
    outs = [[] for _ in range(10)]
    for l in range(depth):
        w_in_l = w_in[l]
        w_in_bf = jnp.concatenate([w_in_l[:, :Q_A][:, perm], w_in_l[:, Q_A:]], axis=1).astype(BF16)
        w_o_bf = jnp.concatenate([w_o[l][:Q_A][perm], w_o[l][Q_A:]], axis=0).astype(BF16)
        w_mem_bf = w_mem_kv[l].astype(BF16)
        wg, wu, wd = w_gate[l].astype(BF16), w_up[l].astype(BF16), w_down[l].astype(BF16)
        g0, g1, g2, g3 = (g[l][None, :] for g in (g_pre_mix, g_post_mix, g_pre_ffn, g_post_ffn))
        sink_heads = sinks[l].reshape(-1)[np.array(SWA_HEAD_ORDER)].astype(F32)
        sink_row = jnp.repeat(sink_heads, HEAD_DIM)[None, :]
        sink_rows = jnp.repeat(sink_heads, s_len)[:, None]

        qa, qb, qx, kah, vah, kbf, vbf, kat, vat, kbt, vbt = _proj(
            hp, g0, w_in_bf, cos_p, sin_p, tm=512, prompt=True, n_a=n_a, n_b=n_b)
        qa_s, qb_s, qx_s, kan, van, kbn, vbn = (a[0] for a in _proj(
            hs, g0, w_in_bf, cos_s, sin_s, tm=tm_s, prompt=False))

        mkt, mvt, mkh, mvh = _memkv(mem_prompt, w_mem_bf)
        oa, ox = _swa_mem(qa, kah, vah, qx, mkh, mvh, sink_row, qt=1024)
        ob = _dil(qb, kbf, vbf, qt=2048)

        to_units = lambda a: jnp.transpose(a.reshape(bs, s_len, n_pairs, LANES), (0, 2, 1, 3)).reshape(
            bs * n_pairs, s_len, LANES)
        cache_units = lambda c: _seq_minor(c).reshape(bs * n_pairs, LANES, lb_b)
        flat = lambda a: a.reshape(bp * t_p, a.shape[-1])
        hp, ob_s, sbk, sbv = _finish_shift(
            flat(hp), flat(oa), flat(ob), flat(ox), w_o_bf, g1, g2, wg, wu, wd, g3,
            to_units(qb_s), to_units(kbn), to_units(vbn), cache_units(cache_dil_k[l]), cache_units(cache_dil_v[l]),
            bcnt, tm=512, s_len=s_len, ff_chunk=ff_chunk)
        hp = hp.reshape(bp, t_p, d)
        ob_s = jnp.transpose(ob_s.reshape(bs, n_pairs, s_len, LANES), (0, 2, 1, 3)).reshape(bs * s_len, Q_B)
        outs[0].append(_seq_major(kat, SWA_KV_HEADS))
        outs[1].append(_seq_major(vat, SWA_KV_HEADS))
        outs[2].append(_seq_major(kbt, DIL_HEADS))
        outs[3].append(_seq_major(vbt, DIL_HEADS))
        outs[4].append(_seq_major(mkt, MEM_HEADS))
        outs[5].append(_seq_major(mvt, MEM_HEADS))

        oa_s, ox_s, sak, sav = _sample_small(
            qa_s, kan, van, qx_s, _seq_minor(cache_swa_k[l]), _seq_minor(cache_swa_v[l]),
            _seq_minor(cache_mem_k[l]), _seq_minor(cache_mem_v[l]), sink_rows, amask, s_len=s_len)
        hs = _finish(hs[0], oa_s, ob_s, ox_s, w_o_bf, g1, g2, wg, wu, wd, g3, tm=256, ff_chunk=ff_chunk)[None]
        outs[6].append(_seq_major(sak, SWA_KV_HEADS))
        outs[7].append(_seq_major(sav, SWA_KV_HEADS))
        outs[8].append(_seq_major(sbk.reshape(bs, KV_B, lb_b), DIL_HEADS))
        outs[9].append(_seq_major(sbv.reshape(bs, KV_B, lb_b), DIL_HEADS))

    return (hp, hs.reshape(bs, s_len, d)) + tuple(jnp.stack(o) for o in outs)
```

```python
import functools

import numpy as np
import jax
import jax.numpy as jnp
from jax import lax
from jax.experimental import pallas as pl
from jax.experimental.pallas import tpu as pltpu

F32 = jnp.float32
BF16 = jnp.bfloat16

HEAD_DIM = 64
SWA_KV_HEADS = 2
SWA_GROUP = 3
SWA_WINDOW = 128
DIL_HEADS = 6
DIL_PATTERNS = ((128, 1), (512, 4), (2048, 16))
DIL_MAX_WINDOW = 2048
MEM_HEADS = 4
BLK = 128
ROPE_THETA = 10000.0
EPS = 1e-6
SCALE = HEAD_DIM ** -0.5
PAST_LEN = 16384

Q_A = SWA_KV_HEADS * SWA_GROUP * HEAD_DIM
KV_A = SWA_KV_HEADS * HEAD_DIM
Q_B = DIL_HEADS * HEAD_DIM
KV_B = DIL_HEADS * HEAD_DIM
Q_X = MEM_HEADS * HEAD_DIM
IN_SIZES = (Q_A, KV_A, KV_A, Q_B, KV_B, KV_B, Q_X)
IN_OFFS = tuple(int(v) for v in np.cumsum((0,) + IN_SIZES))

LANES = 128
VMEM_LIMIT = 56 * 1024 * 1024

SWA_HEAD_ORDER = (0, 3, 1, 4, 2, 5)


def _params(n_axes):
    return pltpu.CompilerParams(dimension_semantics=("arbitrary",) * n_axes,
                                vmem_limit_bytes=VMEM_LIMIT)


def _rms(x, g):
    return x * lax.rsqrt(jnp.mean(x * x, axis=-1, keepdims=True) + EPS) * g


def _lane_lo(shape):
    return lax.broadcasted_iota(jnp.int32, shape, len(shape) - 1) % LANES < HEAD_DIM


def _proj_kernel(x_ref, g_ref, w_ref, cos_ref, sin_ref, *out_refs, prompt, n_a, n_b):
    tm = x_ref.shape[0]
    u = _rms(x_ref[...], g_ref[...]).astype(BF16)
    cos = cos_ref[...]
    sin = sin_ref[...]
    first_half = lax.broadcasted_iota(jnp.int32, cos.shape, 1) % HEAD_DIM < HEAD_DIM // 2

    def proj(i):
        return jnp.dot(u, w_ref[:, IN_OFFS[i]:IN_OFFS[i + 1]], preferred_element_type=F32)

    def rope(z):
        tiles = []
        for s in range(z.shape[1] // LANES):
            zs = z[:, s * LANES:(s + 1) * LANES]
            partner = jnp.where(first_half, pltpu.roll(zs, LANES - HEAD_DIM // 2, 1),
                                pltpu.roll(zs, HEAD_DIM // 2, 1))
            tiles.append(zs * cos + partner * sin)
        return jnp.concatenate(tiles, axis=1) if len(tiles) > 1 else tiles[0]

    qa = rope(proj(0))
    ka = rope(proj(1))
    va = proj(2)
    qb = rope(proj(3))
    kb = rope(proj(4))
    vb = proj(5)
    qx = proj(6)
    qa_ref, qb_ref, qx_ref, ka_ref, va_ref, kb_ref, vb_ref = out_refs[:7]
    qa_ref[...] = (qa * SCALE).astype(qa_ref.dtype)
    qx_ref[...] = (qx * SCALE).astype(qx_ref.dtype)
    ka_ref[...] = ka.astype(ka_ref.dtype)
    va_ref[...] = va.astype(va_ref.dtype)
    if not prompt:
        qb_ref[...] = qb * SCALE
        kb_ref[...] = kb
        vb_ref[...] = vb
    else:
        for p in range(Q_B // LANES):
            cols = slice(p * LANES, (p + 1) * LANES)
            qb_ref[p] = qb[:, cols] * SCALE
            kb_ref[p] = kb[:, cols]
            vb_ref[p] = vb[:, cols]
        kat_ref, vat_ref, kbt_ref, vbt_ref = out_refs[7:]
        j = pl.program_id(1)
        nj = pl.num_programs(1)

        @pl.when(j >= nj - n_b // tm)
        def _():
            kbt_ref[...] = kb.T
            vbt_ref[...] = vb.T

        @pl.when(j == nj - 1)
        def _():
            kat_ref[...] = ka[tm - n_a:, :].T
            vat_ref[...] = va[tm - n_a:, :].T


def _proj(x, g, w_bf, cos, sin, *, tm, prompt, n_a=0, n_b=0):
    b, t, d = x.shape
    nj = t // tm
    n_rope = cos.shape[0] // tm
    tile = lambda c: pl.BlockSpec((None, tm, c), lambda i, j: (i, j, 0))
    rope = pl.BlockSpec((tm, LANES), lambda i, j: (j % n_rope, 0))
    dt = BF16 if prompt else F32
    widths = (Q_A, Q_B, Q_X, KV_A, KV_A, KV_B, KV_B)
    out_specs = [tile(c) for c in widths]
    out_shape = [jax.ShapeDtypeStruct((b, t, c), dt) for c in widths]
    if prompt:
        assert n_a <= tm and n_a % LANES == 0 and n_b % tm == 0 and n_b <= t
        n_pairs = Q_B // LANES
        for i in (1, 5, 6):
            out_specs[i] = pl.BlockSpec((None, n_pairs, tm, LANES), lambda i, j: (i, 0, j, 0))
            out_shape[i] = jax.ShapeDtypeStruct((b, n_pairs, t, LANES), F32)
        first_b = nj - n_b // tm
        out_specs += [pl.BlockSpec((None, KV_A, n_a), lambda i, j: (i, 0, 0))] * 2
        out_specs += [pl.BlockSpec((None, KV_B, tm), lambda i, j: (i, 0, jnp.maximum(j - first_b, 0)))] * 2
        out_shape += [jax.ShapeDtypeStruct((b, KV_A, n_a), F32)] * 2 + [jax.ShapeDtypeStruct((b, KV_B, n_b), F32)] * 2
    return pl.pallas_call(
        functools.partial(_proj_kernel, prompt=prompt, n_a=n_a, n_b=n_b),
        grid=(b, nj),
        in_specs=[tile(d), pl.BlockSpec((1, d), lambda i, j: (0, 0)),
                  pl.BlockSpec(w_bf.shape, lambda i, j: (0, 0), pipeline_mode=pl.Buffered(1)),
                  rope, rope],
        out_specs=out_specs,
        out_shape=out_shape,
        compiler_params=_params(2),
        name="proj_prompt" if prompt else "proj_sample",
    )(x, g, w_bf, cos, sin)


def _memkv_kernel(mem_ref, w_ref, mkt_ref, mvt_ref, mkh_ref, mvh_ref):
    z = jnp.dot(mem_ref[...].astype(BF16), w_ref[...], preferred_element_type=F32)
    mk = z[:, :Q_X]
    mv = z[:, Q_X:]
    mkt_ref[...] = mk.T
    mvt_ref[...] = mv.T
    mkh_ref[...] = mk.astype(BF16)
    mvh_ref[...] = mv.astype(BF16)


def _memkv(mem, w_bf):
    b, m, d = mem.shape
    blk = lambda r, c: pl.BlockSpec((None, r, c), lambda i: (i, 0, 0))
    return pl.pallas_call(
        _memkv_kernel,
        grid=(b,),
        in_specs=[blk(m, d), pl.BlockSpec(w_bf.shape, lambda i: (0, 0))],
        out_specs=[blk(Q_X, m)] * 2 + [blk(m, Q_X)] * 2,
        out_shape=[jax.ShapeDtypeStruct((b, Q_X, m), F32)] * 2 + [jax.ShapeDtypeStruct((b, m, Q_X), BF16)] * 2,
        compiler_params=_params(1),
        name="memkv",
    )(mem, w_bf)


def _pair_attend(q, kk, vv, bias):
    tq = q.shape[0]
    zero = jnp.zeros_like(q)
    q_lo = _lane_lo(q.shape)
    q2 = jnp.concatenate([jnp.where(q_lo, q, zero), jnp.where(q_lo, zero, q)], axis=0)
    s = lax.dot_general(q2, kk, (((1,), (1,)), ((), ())), preferred_element_type=F32)
    stats = []
    for h in range(2):
        sh = s[h * tq:(h + 1) * tq]
        if bias is not None:
            sh = sh + bias
        m = jnp.max(sh, axis=1, keepdims=True)
        p = jnp.exp(sh - m)
        stats.append((m, jnp.sum(p, axis=1, keepdims=True), p.astype(BF16)))
    v_lo = _lane_lo(vv.shape)
    vzero = jnp.zeros_like(vv)
    o = (jnp.dot(stats[0][2], jnp.where(v_lo, vv, vzero), preferred_element_type=F32)
         + jnp.dot(stats[1][2], jnp.where(v_lo, vzero, vv), preferred_element_type=F32))
    o_lo = _lane_lo(o.shape)
    m = jnp.where(o_lo, stats[0][0], stats[1][0])
    l = jnp.where(o_lo, stats[0][1], stats[1][1])
    return o, m, l


def _band_bias():
    qi = np.arange(BLK)[:, None]
    c = np.arange(2 * BLK)[None, :]
    in_prev = (c < BLK) & (c >= qi)
    in_cur = (c >= BLK) & (c - BLK <= qi)
    allowed = np.stack([in_cur, in_prev | in_cur])
    return np.where(allowed, 0.0, -np.inf).astype(np.float32)


def _band_rows(j, i, n_blk):
    n = j * n_blk + i
    r0 = pl.multiple_of(i * BLK, BLK)
    k_cur = pl.multiple_of(n * BLK, BLK)
    k_prev = pl.multiple_of(jnp.maximum(n - 1, 0) * BLK, BLK)
    return n, r0, k_prev, k_cur


def _swa_mem_kernel(qa_ref, ka_ref, va_ref, qx_ref, mk_ref, mv_ref, sink_ref, bias_ref, oa_ref, ox_ref, *, qt):
    j = pl.program_id(1)
    n_blk = qt // BLK

    def body(i, carry):
        n, r0, k_prev, k_cur = _band_rows(j, i, n_blk)
        rows = pl.ds(r0, BLK)
        mask = bias_ref[jnp.minimum(n, 1)]
        kk = jnp.concatenate([ka_ref[pl.ds(k_prev, BLK), :], ka_ref[pl.ds(k_cur, BLK), :]], axis=0)
        vv = jnp.concatenate([va_ref[pl.ds(k_prev, BLK), :], va_ref[pl.ds(k_cur, BLK), :]], axis=0)
        for p in range(Q_A // LANES):
            cols = slice(p * LANES, (p + 1) * LANES)
            o, m, l = _pair_attend(qa_ref[rows, cols], kk, vv, mask)
            sink = sink_ref[:, cols]
            mm = jnp.maximum(m, sink)
            a = jnp.exp(m - mm)
            den = l * a + jnp.exp(sink - mm)
            oa_ref[rows, cols] = (o * (a / den)).astype(oa_ref.dtype)
        for p in range(Q_X // LANES):
            cols = slice(p * LANES, (p + 1) * LANES)
            o, m, l = _pair_attend(qx_ref[rows, cols], mk_ref[:, cols], mv_ref[:, cols], None)
            ox_ref[rows, cols] = (o / l).astype(ox_ref.dtype)
        return carry

    lax.fori_loop(0, n_blk, body, 0)


def _swa_mem(qa, ka, va, qx, mk, mv, sink_row, *, qt):
    b, t, _ = qa.shape
    m = mk.shape[1]
    qtile = lambda c: pl.BlockSpec((None, qt, c), lambda i, j: (i, j, 0))
    whole = lambda r, c: pl.BlockSpec((None, r, c), lambda i, j: (i, 0, 0))
    return pl.pallas_call(
        functools.partial(_swa_mem_kernel, qt=qt),
        grid=(b, t // qt),
        in_specs=[qtile(Q_A), whole(t, KV_A), whole(t, KV_A), qtile(Q_X), whole(m, Q_X), whole(m, Q_X),
                  pl.BlockSpec((1, Q_A), lambda i, j: (0, 0)),
                  pl.BlockSpec((2, BLK, 2 * BLK), lambda i, j: (0, 0, 0))],
        out_specs=[qtile(Q_A), qtile(Q_X)],
        out_shape=[jax.ShapeDtypeStruct((b, t, Q_A), BF16), jax.ShapeDtypeStruct((b, t, Q_X), BF16)],
        compiler_params=_params(2),
        name="swa_mem",
    )(qa, ka, va, qx, mk, mv, sink_row, jnp.asarray(_band_bias()))


def _dil_kernel(q_ref, k_ref, v_ref, bias_ref, ob_ref, *acc, qt):
    j = pl.program_id(1)
    n_pairs = Q_B // LANES
    patterns = sorted(DIL_PATTERNS, key=lambda wr: -wr[1])
    for idx, (w, r) in enumerate(patterns):
        assert w // r == BLK and qt % (r * BLK) == 0
        n_blk = qt // (r * BLK)

        def body(step, carry, r=r, n_blk=n_blk, first=(idx == 0)):
            a = lax.div(step, n_blk)
            i = lax.rem(step, n_blk)
            n = j * n_blk + i

            def rows(blk):
                if r == 1:
                    return pl.ds(pl.multiple_of(blk * BLK, BLK), BLK)
                return pl.ds(blk * (BLK * r) + a, BLK, stride=r)

            prev = jnp.maximum(n - 1, 0)
            bias = bias_ref[jnp.minimum(n, 1)]
            for p in range(n_pairs):
                acc_o, acc_m, acc_l = acc[3 * p:3 * p + 3]
                q = q_ref[p, rows(i), :].astype(BF16)
                kk = jnp.concatenate([k_ref[p, rows(prev), :], k_ref[p, rows(n), :]], axis=0).astype(BF16)
                vv = jnp.concatenate([v_ref[p, rows(prev), :], v_ref[p, rows(n), :]], axis=0).astype(BF16)
                o, m, l = _pair_attend(q, kk, vv, bias)
                if not first:
                    m_old = acc_m[rows(i), :]
                    m_new = jnp.maximum(m_old, m)
                    e_old = jnp.exp(m_old - m_new)
                    e = jnp.exp(m - m_new)
                    o = acc_o[rows(i), :] * e_old + o * e
                    l = acc_l[rows(i), :] * e_old + l * e
                    m = m_new
                acc_o[rows(i), :] = o
                acc_m[rows(i), :] = m
                acc_l[rows(i), :] = l
            return carry

        lax.fori_loop(0, r * n_blk, body, 0, unroll=2)
    for p in range(n_pairs):
        ob_ref[:, p * LANES:(p + 1) * LANES] = (acc[3 * p][...] / acc[3 * p + 2][...]).astype(ob_ref.dtype)


def _dil(q, k, v, *, qt):
    b, n_pairs, t, _ = q.shape
    return pl.pallas_call(
        functools.partial(_dil_kernel, qt=qt),
        grid=(b, t // qt),
        in_specs=[pl.BlockSpec((None, n_pairs, qt, LANES), lambda i, j: (i, 0, j, 0)),
                  pl.BlockSpec((None, n_pairs, t, LANES), lambda i, j: (i, 0, 0, 0)),
                  pl.BlockSpec((None, n_pairs, t, LANES), lambda i, j: (i, 0, 0, 0)),
                  pl.BlockSpec((2, BLK, 2 * BLK), lambda i, j: (0, 0, 0))],
        out_specs=pl.BlockSpec((None, qt, n_pairs * LANES), lambda i, j: (i, j, 0)),
        out_shape=jax.ShapeDtypeStruct((b, t, n_pairs * LANES), BF16),
        scratch_shapes=[pltpu.VMEM((qt, LANES), F32)] * (3 * n_pairs),
        compiler_params=_params(2),
        name="dil",
    )(q, k, v, jnp.asarray(_band_bias()))


def _shift_and_stage(cache_ref, new_ref, out_ref, stage, s_len):
    c, lb = cache_ref.shape
    new = new_ref[...]
    new_t = jnp.concatenate([new, jnp.zeros((LANES - s_len, c), F32)], axis=0).T
    cache = cache_ref[...]
    stage[:, 0:lb] = cache.astype(BF16)
    stage[:, lb:] = new_t.astype(BF16)
    shifted = pltpu.roll(cache, lb - s_len, 1)
    if lb > LANES:
        out_ref[:, 0:lb - LANES] = shifted[:, 0:lb - LANES]
    keep = lax.broadcasted_iota(jnp.int32, (c, LANES), 1) < LANES - s_len
    out_ref[:, lb - LANES:] = jnp.where(keep, shifted[:, lb - LANES:], pltpu.roll(new_t, LANES - s_len, 1))


def _scores_t(qblk, keys_t):
    return jnp.dot(qblk.astype(BF16), keys_t, preferred_element_type=F32)


def _weighted_t(p, values_t):
    return lax.dot_general(p.astype(BF16), values_t, (((1,), (1,)), ((), ())), preferred_element_type=F32)


def _sample_small_kernel(qa_ref, kan_ref, van_ref, qx_ref, cak_ref, cav_ref, cmk_ref, cmv_ref, sink_ref, amask_ref,
                         oa_ref, ox_ref, sak_ref, sav_ref, ka_s, va_s, *, s_len):
    lo = _lane_lo((s_len, LANES))
    for b in range(cak_ref.shape[0]):
        tok = slice(b * s_len, (b + 1) * s_len)
        _shift_and_stage(cak_ref.at[b], kan_ref.at[tok], sak_ref.at[b], ka_s.at[b], s_len)
        _shift_and_stage(cav_ref.at[b], van_ref.at[tok], sav_ref.at[b], va_s.at[b], s_len)

        qa = qa_ref[tok, :]
        pieces = []
        for p in range(Q_A // LANES):
            tile = qa[:, p * LANES:(p + 1) * LANES]
            pieces += [jnp.where(lo, tile, 0.0), jnp.where(lo, 0.0, tile)]
        s = _scores_t(jnp.concatenate(pieces, axis=0), ka_s[b])
        s = jnp.where(amask_ref[...] > 0.0, s, -jnp.inf)
        m = jnp.max(s, axis=1, keepdims=True)
        p_ = jnp.exp(s - m)
        l = jnp.sum(p_, axis=1, keepdims=True)
        o = _weighted_t(p_, va_s[b])
        sink = sink_ref[...]
        mm = jnp.maximum(m, sink)
        a = jnp.exp(m - mm)
        o = o * (a / (l * a + jnp.exp(sink - mm)))
        oa_ref[tok, :] = jnp.concatenate(
            [jnp.where(lo, o[2 * p * s_len:(2 * p + 1) * s_len], o[(2 * p + 1) * s_len:(2 * p + 2) * s_len])
             for p in range(Q_A // LANES)], axis=1)

        qx = jnp.concatenate([qx_ref[tok, :]] * MEM_HEADS, axis=0)
        row_h = lax.broadcasted_iota(jnp.int32, qx.shape, 0) // s_len
        col_h = lax.broadcasted_iota(jnp.int32, qx.shape, 1) // HEAD_DIM
        own = row_h == col_h
        s = _scores_t(jnp.where(own, qx, 0.0), cmk_ref[b].astype(BF16))
        m = jnp.max(s, axis=1, keepdims=True)
        p_ = jnp.exp(s - m)
        l = jnp.sum(p_, axis=1, keepdims=True)
        o_all = jnp.where(own, _weighted_t(p_, cmv_ref[b].astype(BF16)) / l, 0.0)
        o = o_all[0:s_len]
        for h in range(1, MEM_HEADS):
            o = o + o_all[h * s_len:(h + 1) * s_len]
        ox_ref[tok, :] = o


def _sample_tables(s_len, lb_a, lb_b, na, nb):
    i = np.arange(s_len)[:, None]
    dist_a = lb_a + i - np.arange(na)[None, :]
    amask = ((dist_a >= 0) & (dist_a <= SWA_WINDOW)).astype(np.float32)
    dist_b = lb_b + i - np.arange(nb)[None, :]
    cnt = np.zeros(dist_b.shape, np.float32)
    for w, r in DIL_PATTERNS:
        cnt += ((dist_b >= 0) & (dist_b <= w) & (dist_b % r == 0)).astype(np.float32)
    return np.tile(amask, (SWA_KV_HEADS * SWA_GROUP, 1)), np.tile(cnt, (2, 1))


def _sample_small(qa, kan, van, qx, cak, cav, cmk, cmv, sink_rows, amask, *, s_len, group):
    nbat, _, lb_a = cak.shape
    mtok = cmk.shape[2]
    na = amask.shape[1]
    assert nbat % group == 0
    tok = lambda c: pl.BlockSpec((group * s_len, c), lambda n: (n, 0))
    per_b = lambda r, c: pl.BlockSpec((group, r, c), lambda n: (n, 0, 0))
    const = lambda a: pl.BlockSpec(a.shape, lambda n: (0, 0))
    n_tok = nbat * s_len
    return pl.pallas_call(
        functools.partial(_sample_small_kernel, s_len=s_len),
        grid=(nbat // group,),
        in_specs=[tok(Q_A), tok(KV_A), tok(KV_A), tok(Q_X),
                  per_b(KV_A, lb_a), per_b(KV_A, lb_a), per_b(Q_X, mtok), per_b(Q_X, mtok),
                  const(sink_rows), const(amask)],
        out_specs=[tok(Q_A), tok(Q_X), per_b(KV_A, lb_a), per_b(KV_A, lb_a)],
        out_shape=[jax.ShapeDtypeStruct((n_tok, Q_A), F32), jax.ShapeDtypeStruct((n_tok, Q_X), F32),
                   jax.ShapeDtypeStruct(cak.shape, F32), jax.ShapeDtypeStruct(cav.shape, F32)],
        scratch_shapes=[pltpu.VMEM((group, KV_A, na), BF16), pltpu.VMEM((group, KV_A, na), BF16)],
        compiler_params=_params(1),
        name="sample_small",
    )(qa, kan, van, qx, cak, cav, cmk, cmv, sink_rows, amask)


def _dil_unit(qn_ref, kn_ref, vn_ref, ck_ref, cv_ref, cnt_ref, obs_ref, sk_ref, sv_ref, k_s, v_s, s_len):
    _shift_and_stage(ck_ref, kn_ref, sk_ref, k_s, s_len)
    _shift_and_stage(cv_ref, vn_ref, sv_ref, v_s, s_len)
    qn = qn_ref[...]
    lo = _lane_lo(qn.shape)
    s = _scores_t(jnp.concatenate([jnp.where(lo, qn, 0.0), jnp.where(lo, 0.0, qn)], axis=0), k_s[...])
    cnt = cnt_ref[...]
    s = jnp.where(cnt > 0.0, s, -jnp.inf)
    m = jnp.max(s, axis=1, keepdims=True)
    p_ = cnt * jnp.exp(s - m)
    l = jnp.sum(p_, axis=1, keepdims=True)
    o = _weighted_t(p_, v_s[...]) / l
    obs_ref[...] = jnp.where(lo, o[0:s_len], o[s_len:2 * s_len])


def _mix_out(x_ref, oa_ref, ob_ref, ox_ref, wo_ref, g1_ref, g2_ref):
    cat = jnp.concatenate([oa_ref[...].astype(BF16), ob_ref[...].astype(BF16), ox_ref[...].astype(BF16)], axis=1)
    x = x_ref[...] + _rms(jnp.dot(cat, wo_ref[...], preferred_element_type=F32), g1_ref[...])
    return x, _rms(x, g2_ref[...]).astype(BF16)


def _gate_up(h, wg, wu):
    g = jnp.dot(h, wg, preferred_element_type=F32)
    u = jnp.dot(h, wu, preferred_element_type=F32)
    return (g * (1.0 / (1.0 + jnp.exp(-g))) * u).astype(BF16)


def _ffn_chunk(h, wg, wu, wd):
    return jnp.dot(_gate_up(h, wg, wu), wd, preferred_element_type=F32)


def _finish_kernel(x_ref, oa_ref, ob_ref, ox_ref, wo_ref, g1_ref, g2_ref, wg_ref, wu_ref, wd_ref, g3_ref, out_ref,
                   *, ff_chunk):
    x, h = _mix_out(x_ref, oa_ref, ob_ref, ox_ref, wo_ref, g1_ref, g2_ref)
    f = 0.0
    for c0 in range(0, wg_ref.shape[1], ff_chunk):
        cols = slice(c0, c0 + ff_chunk)
        f = f + _ffn_chunk(h, wg_ref[:, cols], wu_ref[:, cols], wd_ref[cols, :])
    out_ref[...] = x + _rms(f, g3_ref[...])


def _finish(x, oa, ob, ox, wo, g1, g2, wg, wu, wd, g3, *, tm, ff_chunk):
    n, d = x.shape
    row = lambda c: pl.BlockSpec((tm, c), lambda i: (i, 0))
    const = lambda a: pl.BlockSpec(a.shape, lambda i: (0,) * a.ndim, pipeline_mode=pl.Buffered(1))
    return pl.pallas_call(
        functools.partial(_finish_kernel, ff_chunk=ff_chunk),
        grid=(n // tm,),
        in_specs=[row(d), row(Q_A), row(Q_B), row(Q_X)] + [const(a) for a in (wo, g1, g2, wg, wu, wd, g3)],
        out_specs=row(d),
        out_shape=jax.ShapeDtypeStruct((n, d), F32),
        compiler_params=_params(1),
        name="finish",
    )(x, oa, ob, ox, wo, g1, g2, wg, wu, wd, g3)


def _finish_shift_kernel(x_ref, oa_ref, ob_ref, ox_ref, wo_ref, g1_ref, g2_ref, wg_ref, wu_ref, wd_ref, g3_ref,
                         qn_ref, kn_ref, vn_ref, ck_hbm, cv_hbm, cnt_ref,
                         out_ref, obs_ref, sk_hbm, sv_hbm,
                         x1_s, h_s, f_s, act_s, k_s, v_s, kin, vin, kout, vout, sem_in, sem_out, *, s_len, ff_chunk):
    c = pl.program_id(1)
    last = pl.num_programs(1) - 1
    step = pl.program_id(0) * pl.num_programs(1) + c
    n_steps = pl.num_programs(0) * pl.num_programs(1)
    r_in, r_out = kin.shape[0], kout.shape[0]
    d_ff = wg_ref.shape[1]
    slot = lax.rem(step, r_in)
    oslot = lax.rem(step, r_out)

    def fetch(u, s):
        return (pltpu.make_async_copy(ck_hbm.at[u], kin.at[s], sem_in.at[0, s]),
                pltpu.make_async_copy(cv_hbm.at[u], vin.at[s], sem_in.at[1, s]))

    def flush(u, s):
        return (pltpu.make_async_copy(kout.at[s], sk_hbm.at[u], sem_out.at[0, s]),
                pltpu.make_async_copy(vout.at[s], sv_hbm.at[u], sem_out.at[1, s]))

    @pl.when(step == 0)
    def _():
        for u in range(r_in):
            for cp in fetch(u, u):
                cp.start()

    for cp in fetch(step, slot):
        cp.wait()

    @pl.when(step >= r_out)
    def _():
        for cp in flush(step - r_out, oslot):
            cp.wait()

    def unit():
        _dil_unit(qn_ref.at[0], kn_ref.at[0], vn_ref.at[0], kin.at[slot], vin.at[slot], cnt_ref,
                  obs_ref.at[0], kout.at[oslot], vout.at[oslot], k_s, v_s, s_len)

    @pl.when(c == 0)
    def _():
        unit()
        x, h = _mix_out(x_ref, oa_ref, ob_ref, ox_ref, wo_ref, g1_ref, g2_ref)
        x1_s[...] = x
        h_s[...] = h
        f_s[...] = jnp.zeros_like(f_s)
        act_s[...] = _gate_up(h, wg_ref[:, 0:ff_chunk], wu_ref[:, 0:ff_chunk])

    @pl.when((c > 0) & (c < last))
    def _():
        unit()
        prev = pl.ds(pl.multiple_of((c - 1) * ff_chunk, ff_chunk), ff_chunk)
        cur = pl.ds(pl.multiple_of(c * ff_chunk, ff_chunk), ff_chunk)
        down = jnp.dot(act_s[...], wd_ref[prev, :], preferred_element_type=F32)
        act_s[...] = _gate_up(h_s[...], wg_ref[:, cur], wu_ref[:, cur])
        f_s[...] += down

    @pl.when(c == last)
    def _():
        unit()
        f = f_s[...] + jnp.dot(act_s[...], wd_ref[d_ff - ff_chunk:d_ff, :], preferred_element_type=F32)
        out_ref[...] = x1_s[...] + _rms(f, g3_ref[...])

    for cp in flush(step, oslot):
        cp.start()

    @pl.when(step + r_in < n_steps)
    def _():
        for cp in fetch(step + r_in, slot):
            cp.start()

    @pl.when(step == n_steps - 1)
    def _():
        for k in range(r_out):
            u = n_steps - 1 - k
            for cp in flush(u, lax.rem(u, r_out)):
                cp.wait()


IN_RING = 3
OUT_RING = 2


def _finish_shift(x, oa, ob, ox, wo, g1, g2, wg, wu, wd, g3, qn, kn, vn, ck, cv, cnt, *, tm, s_len, ff_chunk):
    n, d = x.shape
    n_sub = wg.shape[1] // ff_chunk + 1
    n_units, cw, lb = ck.shape
    assert n_units == (n // tm) * n_sub and n_units >= max(IN_RING, OUT_RING), "one cache unit per grid step"
    row = lambda c: pl.BlockSpec((tm, c), lambda i, c_: (i, 0))
    const = lambda a: pl.BlockSpec(a.shape, lambda i, c_: (0,) * a.ndim, pipeline_mode=pl.Buffered(1))
    unit = pl.BlockSpec((1, s_len, cw), lambda i, c_: (i * n_sub + c_, 0, 0))
    hbm = pl.BlockSpec(memory_space=pl.ANY)
    ring = lambda r: pltpu.VMEM((r, cw, lb), F32)
    return pl.pallas_call(
        functools.partial(_finish_shift_kernel, s_len=s_len, ff_chunk=ff_chunk),
        grid=(n // tm, n_sub),
        in_specs=[row(d), row(Q_A), row(Q_B), row(Q_X)] + [const(a) for a in (wo, g1, g2, wg, wu, wd, g3)]
                 + [unit] * 3 + [hbm] * 2 + [const(cnt)],
        out_specs=[row(d), unit, hbm, hbm],
        out_shape=[jax.ShapeDtypeStruct((n, d), F32), jax.ShapeDtypeStruct(qn.shape, F32),
                   jax.ShapeDtypeStruct(ck.shape, F32), jax.ShapeDtypeStruct(cv.shape, F32)],
        scratch_shapes=[pltpu.VMEM((tm, d), F32), pltpu.VMEM((tm, d), BF16), pltpu.VMEM((tm, d), F32),
                        pltpu.VMEM((tm, ff_chunk), BF16),
                        pltpu.VMEM((cw, lb + LANES), BF16), pltpu.VMEM((cw, lb + LANES), BF16),
                        ring(IN_RING), ring(IN_RING), ring(OUT_RING), ring(OUT_RING),
                        pltpu.SemaphoreType.DMA((2, IN_RING)), pltpu.SemaphoreType.DMA((2, OUT_RING))],
        compiler_params=_params(2),
        name="finish_shift",
    )(x, oa, ob, ox, wo, g1, g2, wg, wu, wd, g3, qn, kn, vn, ck, cv, cnt)


def _rope_tables(pos):
    half = HEAD_DIM // 2
    inv = ROPE_THETA ** (-jnp.arange(half, dtype=F32) / half)
    ang = pos.astype(F32)[:, None] * inv[None, :]
    cos = jnp.tile(jnp.cos(ang), (1, LANES // half))
    sin = jnp.tile(jnp.sin(ang), (1, LANES // half))
    first_half = (jnp.arange(LANES) % HEAD_DIM) < half
    return cos, jnp.where(first_half[None, :], -sin, sin)


def _swa_head_perm():
    return np.concatenate([np.arange(h * HEAD_DIM, (h + 1) * HEAD_DIM) for h in SWA_HEAD_ORDER])


def _seq_minor(c):
    b, n, h, dh = c.shape
    return jnp.transpose(c, (0, 2, 3, 1)).reshape(b, h * dh, n)


def _seq_major(a, heads):
    b, c, n = a.shape
    return jnp.transpose(a.reshape(b, heads, c // heads, n), (0, 3, 1, 2))


def kernel(x_prompt, x_sample, cache_swa_k, cache_swa_v, cache_dil_k, cache_dil_v, cache_mem_k, cache_mem_v,
           mem_prompt, g_pre_mix, w_in, sinks, w_mem_kv, w_o, g_post_mix, g_pre_ffn, w_gate, w_up, w_down,
           g_post_ffn):
    depth = w_in.shape[0]
    bp, t_p, d = x_prompt.shape
    bs, s_len, _ = x_sample.shape
    d_ff = w_gate.shape[2]
    perm = _swa_head_perm()
    cos_p, sin_p = _rope_tables(jnp.arange(t_p))
    cos_s, sin_s = _rope_tables(PAST_LEN + jnp.arange(s_len))
    tm_s = 256
    cos_s = jnp.tile(cos_s, (tm_s // s_len, 1))
    sin_s = jnp.tile(sin_s, (tm_s // s_len, 1))
    ff_chunk = 2 * LANES
    assert d_ff % ff_chunk == 0
    n_pairs = Q_B // LANES
    lb_a = cache_swa_k.shape[2]
    lb_b = cache_dil_k.shape[2]
    assert lb_a >= SWA_WINDOW and lb_b >= DIL_MAX_WINDOW, "every window position must lie inside the cache"
    assert lb_a % LANES == 0 and lb_b % LANES == 0 and s_len <= LANES
    amask, bcnt = (jnp.asarray(a) for a in _sample_tables(s_len, lb_a, lb_b, lb_a + LANES, lb_b + LANES))
    n_a = min(SWA_WINDOW, t_p)
    n_b = min(DIL_MAX_WINDOW, t_p)

    hp = x_prompt
    hs = x_sample.reshape(1, bs * s_len, d)
    outs = [[] for _ in range(10)]
    for l in range(depth):
        w_in_l = w_in[l]
        w_in_bf = jnp.concatenate([w_in_l[:, :Q_A][:, perm], w_in_l[:, Q_A:]], axis=1).astype(BF16)
        w_o_bf = jnp.concatenate([w_o[l][:Q_A][perm], w_o[l][Q_A:]], axis=0).astype(BF16)
        w_mem_bf = w_mem_kv[l].astype(BF16)
        wg, wu, wd = w_gate[l].astype(BF16), w_up[l].astype(BF16), w_down[l].astype(BF16)
        g0, g1, g2, g3 = (g[l][None, :] for g in (g_pre_mix, g_post_mix, g_pre_ffn, g_post_ffn))
        sink_heads = sinks[l].reshape(-1)[np.array(SWA_HEAD_ORDER)].astype(F32)
        sink_row = jnp.repeat(sink_heads, HEAD_DIM)[None, :]
        sink_rows = jnp.repeat(sink_heads, s_len)[:, None]

        qa, qb, qx, kah, vah, kbf, vbf, kat, vat, kbt, vbt = _proj(
            hp, g0, w_in_bf, cos_p, sin_p, tm=512, prompt=True, n_a=n_a, n_b=n_b)
        qa_s, qb_s, qx_s, kan, van, kbn, vbn = (a[0] for a in _proj(
            hs, g0, w_in_bf, cos_s, sin_s, tm=tm_s, prompt=False))

        mkt, mvt, mkh, mvh = _memkv(mem_prompt, w_mem_bf)
        oa, ox = _swa_mem(qa, kah, vah, qx, mkh, mvh, sink_row, qt=1024)
        ob = _dil(qb, kbf, vbf, qt=2048)

        to_units = lambda a: jnp.transpose(a.reshape(bs, s_len, n_pairs, LANES), (0, 2, 1, 3)).reshape(
            bs * n_pairs, s_len, LANES)
        cache_units = lambda c: _seq_minor(c).reshape(bs * n_pairs, LANES, lb_b)
        flat = lambda a: a.reshape(bp * t_p, a.shape[-1])
        hp, ob_s, sbk, sbv = _finish_shift(
            flat(hp), flat(oa), flat(ob), flat(ox), w_o_bf, g1, g2, wg, wu, wd, g3,
            to_units(qb_s), to_units(kbn), to_units(vbn), cache_units(cache_dil_k[l]), cache_units(cache_dil_v[l]),
            bcnt, tm=512, s_len=s_len, ff_chunk=ff_chunk)
        hp = hp.reshape(bp, t_p, d)
        ob_s = jnp.transpose(ob_s.reshape(bs, n_pairs, s_len, LANES), (0, 2, 1, 3)).reshape(bs * s_len, Q_B)
        outs[0].append(_seq_major(kat, SWA_KV_HEADS))
        outs[1].append(_seq_major(vat, SWA_KV_HEADS))
        outs[2].append(_seq_major(kbt, DIL_HEADS))
        outs[3].append(_seq_major(vbt, DIL_HEADS))
        outs[4].append(_seq_major(mkt, MEM_HEADS))
        outs[5].append(_seq_major(mvt, MEM_HEADS))

        oa_s, ox_s, sak, sav = _sample_small(
            qa_s, kan, van, qx_s, _seq_minor(cache_swa_k[l]), _seq_minor(cache_swa_v[l]),
            _seq_minor(cache_mem_k[l]), _seq_minor(cache_mem_v[l]), sink_rows, amask, s_len=s_len, group=8)
        hs = _finish(hs[0], oa_s, ob_s, ox_s, w_o_bf, g1, g2, wg, wu, wd, g3, tm=256, ff_chunk=ff_chunk)[None]
        outs[6].append(_seq_major(sak, SWA_KV_HEADS))
        outs[7].append(_seq_major(sav, SWA_KV_HEADS))
        outs[8].append(_seq_major(sbk.reshape(bs, KV_B, lb_b), DIL_HEADS))
        outs[9].append(_seq_major(sbv.reshape(bs, KV_B, lb_b), DIL_HEADS))

    return (hp, hs.reshape(bs, s_len, d)) + tuple(jnp.stack(o) for o in outs)
```

```python
import functools

import numpy as np
import jax
import jax.numpy as jnp
from jax import lax
from jax.experimental import pallas as pl
from jax.experimental.pallas import tpu as pltpu

F32 = jnp.float32
BF16 = jnp.bfloat16

HEAD_DIM = 64
SWA_KV_HEADS = 2
SWA_GROUP = 3
SWA_WINDOW = 128
DIL_HEADS = 6
DIL_PATTERNS = ((128, 1), (512, 4), (2048, 16))
DIL_MAX_WINDOW = 2048
MEM_HEADS = 4
BLK = 128
ROPE_THETA = 10000.0
EPS = 1e-6
SCALE = HEAD_DIM ** -0.5
PAST_LEN = 16384

Q_A = SWA_KV_HEADS * SWA_GROUP * HEAD_DIM
KV_A = SWA_KV_HEADS * HEAD_DIM
Q_B = DIL_HEADS * HEAD_DIM
KV_B = DIL_HEADS * HEAD_DIM
Q_X = MEM_HEADS * HEAD_DIM
IN_SIZES = (Q_A, KV_A, KV_A, Q_B, KV_B, KV_B, Q_X)
IN_OFFS = tuple(int(v) for v in np.cumsum((0,) + IN_SIZES))

LANES = 128
VMEM_LIMIT = 56 * 1024 * 1024

SWA_HEAD_ORDER = (0, 3, 1, 4, 2, 5)


def _params(n_axes):
    return pltpu.CompilerParams(dimension_semantics=("arbitrary",) * n_axes,
                                vmem_limit_bytes=VMEM_LIMIT)


def _rms(x, g):
    return x * lax.rsqrt(jnp.mean(x * x, axis=-1, keepdims=True) + EPS) * g


def _lane_lo(shape):
    return lax.broadcasted_iota(jnp.int32, shape, len(shape) - 1) % LANES < HEAD_DIM


def _proj_kernel(x_ref, g_ref, w_ref, cos_ref, sin_ref, *out_refs, prompt, n_a, n_b):
    tm = x_ref.shape[0]
    u = _rms(x_ref[...], g_ref[...]).astype(BF16)
    cos = cos_ref[...]
    sin = sin_ref[...]
    first_half = lax.broadcasted_iota(jnp.int32, cos.shape, 1) % HEAD_DIM < HEAD_DIM // 2

    z = jnp.dot(u, w_ref[...], preferred_element_type=F32)

    def proj(i):
        return z[:, IN_OFFS[i]:IN_OFFS[i + 1]]

    def rope(z):
        tiles = []
        for s in range(z.shape[1] // LANES):
            zs = z[:, s * LANES:(s + 1) * LANES]
            partner = jnp.where(first_half, pltpu.roll(zs, LANES - HEAD_DIM // 2, 1),
                                pltpu.roll(zs, HEAD_DIM // 2, 1))
            tiles.append(zs * cos + partner * sin)
        return jnp.concatenate(tiles, axis=1) if len(tiles) > 1 else tiles[0]

    qa = rope(proj(0))
    ka = rope(proj(1))
    va = proj(2)
    qb = rope(proj(3))
    kb = rope(proj(4))
    vb = proj(5)
    qx = proj(6)
    qa_ref, qb_ref, qx_ref, ka_ref, va_ref, kb_ref, vb_ref = out_refs[:7]
    qa_ref[...] = (qa * SCALE).astype(qa_ref.dtype)
    qx_ref[...] = (qx * SCALE).astype(qx_ref.dtype)
    ka_ref[...] = ka.astype(ka_ref.dtype)
    va_ref[...] = va.astype(va_ref.dtype)
    if not prompt:
        qb_ref[...] = qb * SCALE
        kb_ref[...] = kb
        vb_ref[...] = vb
    else:
        for p in range(Q_B // LANES):
            cols = slice(p * LANES, (p + 1) * LANES)
            qb_ref[p] = qb[:, cols] * SCALE
            kb_ref[p] = kb[:, cols]
            vb_ref[p] = vb[:, cols]
        kat_ref, vat_ref, kbt_ref, vbt_ref = out_refs[7:]
        j = pl.program_id(1)
        nj = pl.num_programs(1)

        @pl.when(j >= nj - n_b // tm)
        def _():
            kbt_ref[...] = kb.T
            vbt_ref[...] = vb.T

        @pl.when(j == nj - 1)
        def _():
            kat_ref[...] = ka[tm - n_a:, :].T
            vat_ref[...] = va[tm - n_a:, :].T


def _proj(x, g, w_bf, cos, sin, *, tm, prompt, n_a=0, n_b=0):
    b, t, d = x.shape
    nj = t // tm
    n_rope = cos.shape[0] // tm
    tile = lambda c: pl.BlockSpec((None, tm, c), lambda i, j: (i, j, 0))
    rope = pl.BlockSpec((tm, LANES), lambda i, j: (j % n_rope, 0))
    dt = BF16 if prompt else F32
    widths = (Q_A, Q_B, Q_X, KV_A, KV_A, KV_B, KV_B)
    out_specs = [tile(c) for c in widths]
    out_shape = [jax.ShapeDtypeStruct((b, t, c), dt) for c in widths]
    if prompt:
        assert n_a <= tm and n_a % LANES == 0 and n_b % tm == 0 and n_b <= t
        n_pairs = Q_B // LANES
        for i in (1, 5, 6):
            out_specs[i] = pl.BlockSpec((None, n_pairs, tm, LANES), lambda i, j: (i, 0, j, 0))
            out_shape[i] = jax.ShapeDtypeStruct((b, n_pairs, t, LANES), F32)
        first_b = nj - n_b // tm
        out_specs += [pl.BlockSpec((None, KV_A, n_a), lambda i, j: (i, 0, 0))] * 2
        out_specs += [pl.BlockSpec((None, KV_B, tm), lambda i, j: (i, 0, jnp.maximum(j - first_b, 0)))] * 2
        out_shape += [jax.ShapeDtypeStruct((b, KV_A, n_a), F32)] * 2 + [jax.ShapeDtypeStruct((b, KV_B, n_b), F32)] * 2
    return pl.pallas_call(
        functools.partial(_proj_kernel, prompt=prompt, n_a=n_a, n_b=n_b),
        grid=(b, nj),
        in_specs=[tile(d), pl.BlockSpec((1, d), lambda i, j: (0, 0)),
                  pl.BlockSpec(w_bf.shape, lambda i, j: (0, 0), pipeline_mode=pl.Buffered(1)),
                  rope, rope],
        out_specs=out_specs,
        out_shape=out_shape,
        compiler_params=_params(2),
        name="proj_prompt" if prompt else "proj_sample",
    )(x, g, w_bf, cos, sin)


def _memkv_kernel(mem_ref, w_ref, mkt_ref, mvt_ref, mkh_ref, mvh_ref):
    z = jnp.dot(mem_ref[...].astype(BF16), w_ref[...], preferred_element_type=F32)
    mk = z[:, :Q_X]
    mv = z[:, Q_X:]
    mkt_ref[...] = mk.T
    mvt_ref[...] = mv.T
    mkh_ref[...] = mk.astype(BF16)
    mvh_ref[...] = mv.astype(BF16)


def _memkv(mem, w_bf):
    b, m, d = mem.shape
    blk = lambda r, c: pl.BlockSpec((None, r, c), lambda i: (i, 0, 0))
    return pl.pallas_call(
        _memkv_kernel,
        grid=(b,),
        in_specs=[blk(m, d), pl.BlockSpec(w_bf.shape, lambda i: (0, 0))],
        out_specs=[blk(Q_X, m)] * 2 + [blk(m, Q_X)] * 2,
        out_shape=[jax.ShapeDtypeStruct((b, Q_X, m), F32)] * 2 + [jax.ShapeDtypeStruct((b, m, Q_X), BF16)] * 2,
        compiler_params=_params(1),
        name="memkv",
    )(mem, w_bf)


def _pair_attend(q, kk, vv, bias):
    tq = q.shape[0]
    zero = jnp.zeros_like(q)
    q_lo = _lane_lo(q.shape)
    q2 = jnp.concatenate([jnp.where(q_lo, q, zero), jnp.where(q_lo, zero, q)], axis=0)
    s = lax.dot_general(q2, kk, (((1,), (1,)), ((), ())), preferred_element_type=F32)
    stats = []
    for h in range(2):
        sh = s[h * tq:(h + 1) * tq]
        if bias is not None:
            sh = sh + bias
        m = jnp.max(sh, axis=1, keepdims=True)
        p = jnp.exp(sh - m)
        stats.append((m, jnp.sum(p, axis=1, keepdims=True), p.astype(BF16)))
    v_lo = _lane_lo(vv.shape)
    vzero = jnp.zeros_like(vv)
    o = (jnp.dot(stats[0][2], jnp.where(v_lo, vv, vzero), preferred_element_type=F32)
         + jnp.dot(stats[1][2], jnp.where(v_lo, vzero, vv), preferred_element_type=F32))
    o_lo = _lane_lo(o.shape)
    m = jnp.where(o_lo, stats[0][0], stats[1][0])
    l = jnp.where(o_lo, stats[0][1], stats[1][1])
    return o, m, l


def _band_bias():
    qi = np.arange(BLK)[:, None]
    c = np.arange(2 * BLK)[None, :]
    in_prev = (c < BLK) & (c >= qi)
    in_cur = (c >= BLK) & (c - BLK <= qi)
    allowed = np.stack([in_cur, in_prev | in_cur])
    return np.where(allowed, 0.0, -np.inf).astype(np.float32)


def _band_rows(j, i, n_blk):
    n = j * n_blk + i
    r0 = pl.multiple_of(i * BLK, BLK)
    k_cur = pl.multiple_of(n * BLK, BLK)
    k_prev = pl.multiple_of(jnp.maximum(n - 1, 0) * BLK, BLK)
    return n, r0, k_prev, k_cur


def _swa_mem_kernel(qa_ref, ka_ref, va_ref, qx_ref, mk_ref, mv_ref, sink_ref, bias_ref, oa_ref, ox_ref, *, qt):
    j = pl.program_id(1)
    n_blk = qt // BLK

    def body(i, carry):
        n, r0, k_prev, k_cur = _band_rows(j, i, n_blk)
        rows = pl.ds(r0, BLK)
        mask = bias_ref[jnp.minimum(n, 1)]
        kk = jnp.concatenate([ka_ref[pl.ds(k_prev, BLK), :], ka_ref[pl.ds(k_cur, BLK), :]], axis=0)
        vv = jnp.concatenate([va_ref[pl.ds(k_prev, BLK), :], va_ref[pl.ds(k_cur, BLK), :]], axis=0)
        for p in range(Q_A // LANES):
            cols = slice(p * LANES, (p + 1) * LANES)
            o, m, l = _pair_attend(qa_ref[rows, cols], kk, vv, mask)
            sink = sink_ref[:, cols]
            mm = jnp.maximum(m, sink)
            a = jnp.exp(m - mm)
            den = l * a + jnp.exp(sink - mm)
            oa_ref[rows, cols] = (o * (a / den)).astype(oa_ref.dtype)
        for p in range(Q_X // LANES):
            cols = slice(p * LANES, (p + 1) * LANES)
            o, m, l = _pair_attend(qx_ref[rows, cols], mk_ref[:, cols], mv_ref[:, cols], None)
            ox_ref[rows, cols] = (o / l).astype(ox_ref.dtype)
        return carry

    lax.fori_loop(0, n_blk, body, 0, unroll=2)


def _swa_mem(qa, ka, va, qx, mk, mv, sink_row, *, qt):
    b, t, _ = qa.shape
    m = mk.shape[1]
    qtile = lambda c: pl.BlockSpec((None, qt, c), lambda i, j: (i, j, 0))
    whole = lambda r, c: pl.BlockSpec((None, r, c), lambda i, j: (i, 0, 0))
    return pl.pallas_call(
        functools.partial(_swa_mem_kernel, qt=qt),
        grid=(b, t // qt),
        in_specs=[qtile(Q_A), whole(t, KV_A), whole(t, KV_A), qtile(Q_X), whole(m, Q_X), whole(m, Q_X),
                  pl.BlockSpec((1, Q_A), lambda i, j: (0, 0)),
                  pl.BlockSpec((2, BLK, 2 * BLK), lambda i, j: (0, 0, 0))],
        out_specs=[qtile(Q_A), qtile(Q_X)],
        out_shape=[jax.ShapeDtypeStruct((b, t, Q_A), BF16), jax.ShapeDtypeStruct((b, t, Q_X), BF16)],
        compiler_params=_params(2),
        name="swa_mem",
    )(qa, ka, va, qx, mk, mv, sink_row, jnp.asarray(_band_bias()))


def _dil_kernel(q_ref, k_ref, v_ref, bias_ref, ob_ref, *acc, qt):
    j = pl.program_id(1)
    n_pairs = Q_B // LANES
    patterns = sorted(DIL_PATTERNS, key=lambda wr: -wr[1])
    for idx, (w, r) in enumerate(patterns):
        assert w // r == BLK and qt % (r * BLK) == 0
        n_blk = qt // (r * BLK)

        def body(step, carry, r=r, n_blk=n_blk, first=(idx == 0)):
            a = lax.div(step, n_blk)
            i = lax.rem(step, n_blk)
            n = j * n_blk + i

            def rows(blk):
                if r == 1:
                    return pl.ds(pl.multiple_of(blk * BLK, BLK), BLK)
                return pl.ds(blk * (BLK * r) + a, BLK, stride=r)

            prev = jnp.maximum(n - 1, 0)
            bias = bias_ref[jnp.minimum(n, 1)]
            for p in range(n_pairs):
                acc_o, acc_m, acc_l = acc[3 * p:3 * p + 3]
                q = q_ref[p, rows(i), :].astype(BF16)
                kk = jnp.concatenate([k_ref[p, rows(prev), :], k_ref[p, rows(n), :]], axis=0).astype(BF16)
                vv = jnp.concatenate([v_ref[p, rows(prev), :], v_ref[p, rows(n), :]], axis=0).astype(BF16)
                o, m, l = _pair_attend(q, kk, vv, bias)
                if not first:
                    m_old = acc_m[rows(i), :]
                    m_new = jnp.maximum(m_old, m)
                    e_old = jnp.exp(m_old - m_new)
                    e = jnp.exp(m - m_new)
                    o = acc_o[rows(i), :] * e_old + o * e
                    l = acc_l[rows(i), :] * e_old + l * e
                    m = m_new
                acc_o[rows(i), :] = o
                acc_m[rows(i), :] = m
                acc_l[rows(i), :] = l
            return carry

        lax.fori_loop(0, r * n_blk, body, 0, unroll=2)
    for p in range(n_pairs):
        ob_ref[:, p * LANES:(p + 1) * LANES] = (acc[3 * p][...] / acc[3 * p + 2][...]).astype(ob_ref.dtype)


def _dil(q, k, v, *, qt):
    b, n_pairs, t, _ = q.shape
    return pl.pallas_call(
        functools.partial(_dil_kernel, qt=qt),
        grid=(b, t // qt),
        in_specs=[pl.BlockSpec((None, n_pairs, qt, LANES), lambda i, j: (i, 0, j, 0)),
                  pl.BlockSpec((None, n_pairs, t, LANES), lambda i, j: (i, 0, 0, 0)),
                  pl.BlockSpec((None, n_pairs, t, LANES), lambda i, j: (i, 0, 0, 0)),
                  pl.BlockSpec((2, BLK, 2 * BLK), lambda i, j: (0, 0, 0))],
        out_specs=pl.BlockSpec((None, qt, n_pairs * LANES), lambda i, j: (i, j, 0)),
        out_shape=jax.ShapeDtypeStruct((b, t, n_pairs * LANES), BF16),
        scratch_shapes=[pltpu.VMEM((qt, LANES), F32)] * (3 * n_pairs),
        compiler_params=_params(2),
        name="dil",
    )(q, k, v, jnp.asarray(_band_bias()))


def _shift_and_stage(cache_ref, new_ref, out_ref, stage, s_len):
    c, lb = cache_ref.shape
    new = new_ref[...]
    new_t = jnp.concatenate([new, jnp.zeros((LANES - s_len, c), F32)], axis=0).T
    cache = cache_ref[...]
    stage[:, 0:lb] = cache.astype(BF16)
    stage[:, lb:] = new_t.astype(BF16)
    shifted = pltpu.roll(cache, lb - s_len, 1)
    if lb > LANES:
        out_ref[:, 0:lb - LANES] = shifted[:, 0:lb - LANES]
    keep = lax.broadcasted_iota(jnp.int32, (c, LANES), 1) < LANES - s_len
    out_ref[:, lb - LANES:] = jnp.where(keep, shifted[:, lb - LANES:], pltpu.roll(new_t, LANES - s_len, 1))


def _scores_t(qblk, keys_t):
    return jnp.dot(qblk.astype(BF16), keys_t, preferred_element_type=F32)


def _weighted_t(p, values_t):
    return lax.dot_general(p.astype(BF16), values_t, (((1,), (1,)), ((), ())), preferred_element_type=F32)


def _sample_small_kernel(qa_ref, kan_ref, van_ref, qx_ref, cak_ref, cav_ref, cmk_ref, cmv_ref, sink_ref, amask_ref,
                         oa_ref, ox_ref, sak_ref, sav_ref, ka_s, va_s, *, s_len):
    lo = _lane_lo((s_len, LANES))
    for b in range(cak_ref.shape[0]):
        tok = slice(b * s_len, (b + 1) * s_len)
        _shift_and_stage(cak_ref.at[b], kan_ref.at[tok], sak_ref.at[b], ka_s.at[b], s_len)
        _shift_and_stage(cav_ref.at[b], van_ref.at[tok], sav_ref.at[b], va_s.at[b], s_len)

        qa = qa_ref[tok, :]
        pieces = []
        for p in range(Q_A // LANES):
            tile = qa[:, p * LANES:(p + 1) * LANES]
            pieces += [jnp.where(lo, tile, 0.0), jnp.where(lo, 0.0, tile)]
        s = _scores_t(jnp.concatenate(pieces, axis=0), ka_s[b])
        s = jnp.where(amask_ref[...] > 0.0, s, -jnp.inf)
        m = jnp.max(s, axis=1, keepdims=True)
        p_ = jnp.exp(s - m)
        l = jnp.sum(p_, axis=1, keepdims=True)
        o = _weighted_t(p_, va_s[b])
        sink = sink_ref[...]
        mm = jnp.maximum(m, sink)
        a = jnp.exp(m - mm)
        o = o * (a / (l * a + jnp.exp(sink - mm)))
        oa_ref[tok, :] = jnp.concatenate(
            [jnp.where(lo, o[2 * p * s_len:(2 * p + 1) * s_len], o[(2 * p + 1) * s_len:(2 * p + 2) * s_len])
             for p in range(Q_A // LANES)], axis=1)

        qx = jnp.concatenate([qx_ref[tok, :]] * MEM_HEADS, axis=0)
        row_h = lax.broadcasted_iota(jnp.int32, qx.shape, 0) // s_len
        col_h = lax.broadcasted_iota(jnp.int32, qx.shape, 1) // HEAD_DIM
        own = row_h == col_h
        s = _scores_t(jnp.where(own, qx, 0.0), cmk_ref[b].astype(BF16))
        m = jnp.max(s, axis=1, keepdims=True)
        p_ = jnp.exp(s - m)
        l = jnp.sum(p_, axis=1, keepdims=True)
        o_all = jnp.where(own, _weighted_t(p_, cmv_ref[b].astype(BF16)) / l, 0.0)
        o = o_all[0:s_len]
        for h in range(1, MEM_HEADS):
            o = o + o_all[h * s_len:(h + 1) * s_len]
        ox_ref[tok, :] = o


def _sample_tables(s_len, lb_a, lb_b, na, nb):
    i = np.arange(s_len)[:, None]
    dist_a = lb_a + i - np.arange(na)[None, :]
    amask = ((dist_a >= 0) & (dist_a <= SWA_WINDOW)).astype(np.float32)
    dist_b = lb_b + i - np.arange(nb)[None, :]
    cnt = np.zeros(dist_b.shape, np.float32)
    for w, r in DIL_PATTERNS:
        cnt += ((dist_b >= 0) & (dist_b <= w) & (dist_b % r == 0)).astype(np.float32)
    return np.tile(amask, (SWA_KV_HEADS * SWA_GROUP, 1)), np.tile(cnt, (2, 1))


def _sample_small(qa, kan, van, qx, cak, cav, cmk, cmv, sink_rows, amask, *, s_len, group):
    nbat, _, lb_a = cak.shape
    mtok = cmk.shape[2]
    na = amask.shape[1]
    assert nbat % group == 0
    tok = lambda c: pl.BlockSpec((group * s_len, c), lambda n: (n, 0))
    per_b = lambda r, c: pl.BlockSpec((group, r, c), lambda n: (n, 0, 0))
    const = lambda a: pl.BlockSpec(a.shape, lambda n: (0, 0))
    n_tok = nbat * s_len
    return pl.pallas_call(
        functools.partial(_sample_small_kernel, s_len=s_len),
        grid=(nbat // group,),
        in_specs=[tok(Q_A), tok(KV_A), tok(KV_A), tok(Q_X),
                  per_b(KV_A, lb_a), per_b(KV_A, lb_a), per_b(Q_X, mtok), per_b(Q_X, mtok),
                  const(sink_rows), const(amask)],
        out_specs=[tok(Q_A), tok(Q_X), per_b(KV_A, lb_a), per_b(KV_A, lb_a)],
        out_shape=[jax.ShapeDtypeStruct((n_tok, Q_A), F32), jax.ShapeDtypeStruct((n_tok, Q_X), F32),
                   jax.ShapeDtypeStruct(cak.shape, F32), jax.ShapeDtypeStruct(cav.shape, F32)],
        scratch_shapes=[pltpu.VMEM((group, KV_A, na), BF16), pltpu.VMEM((group, KV_A, na), BF16)],
        compiler_params=_params(1),
        name="sample_small",
    )(qa, kan, van, qx, cak, cav, cmk, cmv, sink_rows, amask)


def _dil_unit(qn_ref, kn_ref, vn_ref, ck_ref, cv_ref, cnt_ref, obs_ref, sk_ref, sv_ref, k_s, v_s, s_len):
    _shift_and_stage(ck_ref, kn_ref, sk_ref, k_s, s_len)
    _shift_and_stage(cv_ref, vn_ref, sv_ref, v_s, s_len)
    qn = qn_ref[...]
    lo = _lane_lo(qn.shape)
    s = _scores_t(jnp.concatenate([jnp.where(lo, qn, 0.0), jnp.where(lo, 0.0, qn)], axis=0), k_s[...])
    cnt = cnt_ref[...]
    s = jnp.where(cnt > 0.0, s, -jnp.inf)
    m = jnp.max(s, axis=1, keepdims=True)
    p_ = cnt * jnp.exp(s - m)
    l = jnp.sum(p_, axis=1, keepdims=True)
    o = _weighted_t(p_, v_s[...]) / l
    obs_ref[...] = jnp.where(lo, o[0:s_len], o[s_len:2 * s_len])


def _mix_out(x_ref, oa_ref, ob_ref, ox_ref, wo_ref, g1_ref, g2_ref):
    cat = jnp.concatenate([oa_ref[...].astype(BF16), ob_ref[...].astype(BF16), ox_ref[...].astype(BF16)], axis=1)
    x = x_ref[...] + _rms(jnp.dot(cat, wo_ref[...], preferred_element_type=F32), g1_ref[...])
    return x, _rms(x, g2_ref[...]).astype(BF16)


def _gate_up(h, wg, wu):
    g = jnp.dot(h, wg, preferred_element_type=F32)
    u = jnp.dot(h, wu, preferred_element_type=F32)
    return (g * (1.0 / (1.0 + jnp.exp(-g))) * u).astype(BF16)


def _ffn_chunk(h, wg, wu, wd):
    return jnp.dot(_gate_up(h, wg, wu), wd, preferred_element_type=F32)


def _finish_kernel(x_ref, oa_ref, ob_ref, ox_ref, wo_ref, g1_ref, g2_ref, wg_ref, wu_ref, wd_ref, g3_ref, out_ref,
                   *, ff_chunk):
    x, h = _mix_out(x_ref, oa_ref, ob_ref, ox_ref, wo_ref, g1_ref, g2_ref)
    f = 0.0
    for c0 in range(0, wg_ref.shape[1], ff_chunk):
        cols = slice(c0, c0 + ff_chunk)
        f = f + _ffn_chunk(h, wg_ref[:, cols], wu_ref[:, cols], wd_ref[cols, :])
    out_ref[...] = x + _rms(f, g3_ref[...])


def _finish(x, oa, ob, ox, wo, g1, g2, wg, wu, wd, g3, *, tm, ff_chunk):
    n, d = x.shape
    row = lambda c: pl.BlockSpec((tm, c), lambda i: (i, 0))
    const = lambda a: pl.BlockSpec(a.shape, lambda i: (0,) * a.ndim, pipeline_mode=pl.Buffered(1))
    return pl.pallas_call(
        functools.partial(_finish_kernel, ff_chunk=ff_chunk),
        grid=(n // tm,),
        in_specs=[row(d), row(Q_A), row(Q_B), row(Q_X)] + [const(a) for a in (wo, g1, g2, wg, wu, wd, g3)],
        out_specs=row(d),
        out_shape=jax.ShapeDtypeStruct((n, d), F32),
        compiler_params=_params(1),
        name="finish",
    )(x, oa, ob, ox, wo, g1, g2, wg, wu, wd, g3)


def _finish_shift_kernel(x_ref, oa_ref, ob_ref, ox_ref, wo_ref, g1_ref, g2_ref, wg_ref, wu_ref, wd_ref, g3_ref,
                         qn_ref, kn_ref, vn_ref, ck_hbm, cv_hbm, cnt_ref,
                         out_ref, obs_ref, sk_hbm, sv_hbm,
                         x1_s, h_s, f_s, act_s, k_s, v_s, kin, vin, kout, vout, sem_in, sem_out, *, s_len, ff_chunk):
    c = pl.program_id(1)
    last = pl.num_programs(1) - 1
    step = pl.program_id(0) * pl.num_programs(1) + c
    n_steps = pl.num_programs(0) * pl.num_programs(1)
    r_in, r_out = kin.shape[0], kout.shape[0]
    d_ff = wg_ref.shape[1]
    slot = lax.rem(step, r_in)
    oslot = lax.rem(step, r_out)

    def fetch(u, s):
        return (pltpu.make_async_copy(ck_hbm.at[u], kin.at[s], sem_in.at[0, s]),
                pltpu.make_async_copy(cv_hbm.at[u], vin.at[s], sem_in.at[1, s]))

    def flush(u, s):
        return (pltpu.make_async_copy(kout.at[s], sk_hbm.at[u], sem_out.at[0, s]),
                pltpu.make_async_copy(vout.at[s], sv_hbm.at[u], sem_out.at[1, s]))

    @pl.when(step == 0)
    def _():
        for u in range(r_in):
            for cp in fetch(u, u):
                cp.start()

    for cp in fetch(step, slot):
        cp.wait()

    @pl.when(step >= r_out)
    def _():
        for cp in flush(step - r_out, oslot):
            cp.wait()

    def unit():
        _dil_unit(qn_ref.at[0], kn_ref.at[0], vn_ref.at[0], kin.at[slot], vin.at[slot], cnt_ref,
                  obs_ref.at[0], kout.at[oslot], vout.at[oslot], k_s, v_s, s_len)

    @pl.when(c == 0)
    def _():
        unit()
        x, h = _mix_out(x_ref, oa_ref, ob_ref, ox_ref, wo_ref, g1_ref, g2_ref)
        x1_s[...] = x
        h_s[...] = h
        f_s[...] = jnp.zeros_like(f_s)
        act_s[...] = _gate_up(h, wg_ref[:, 0:ff_chunk], wu_ref[:, 0:ff_chunk])

    @pl.when((c > 0) & (c < last))
    def _():
        unit()
        prev = pl.ds(pl.multiple_of((c - 1) * ff_chunk, ff_chunk), ff_chunk)
        cur = pl.ds(pl.multiple_of(c * ff_chunk, ff_chunk), ff_chunk)
        down = jnp.dot(act_s[...], wd_ref[prev, :], preferred_element_type=F32)
        act_s[...] = _gate_up(h_s[...], wg_ref[:, cur], wu_ref[:, cur])
        f_s[...] += down

    @pl.when(c == last)
    def _():
        unit()
        f = f_s[...] + jnp.dot(act_s[...], wd_ref[d_ff - ff_chunk:d_ff, :], preferred_element_type=F32)
        out_ref[...] = x1_s[...] + _rms(f, g3_ref[...])

    for cp in flush(step, oslot):
        cp.start()

    @pl.when(step + r_in < n_steps)
    def _():
        for cp in fetch(step + r_in, slot):
            cp.start()

    @pl.when(step == n_steps - 1)
    def _():
        for k in range(r_out):
            u = n_steps - 1 - k
            for cp in flush(u, lax.rem(u, r_out)):
                cp.wait()


IN_RING = 4
OUT_RING = 3


def _finish_shift(x, oa, ob, ox, wo, g1, g2, wg, wu, wd, g3, qn, kn, vn, ck, cv, cnt, *, tm, s_len, ff_chunk):
    n, d = x.shape
    n_sub = wg.shape[1] // ff_chunk + 1
    n_units, cw, lb = ck.shape
    assert n_units == (n // tm) * n_sub and n_units >= max(IN_RING, OUT_RING), "one cache unit per grid step"
    row = lambda c: pl.BlockSpec((tm, c), lambda i, c_: (i, 0))
    const = lambda a: pl.BlockSpec(a.shape, lambda i, c_: (0,) * a.ndim, pipeline_mode=pl.Buffered(1))
    unit = pl.BlockSpec((1, s_len, cw), lambda i, c_: (i * n_sub + c_, 0, 0))
    hbm = pl.BlockSpec(memory_space=pl.ANY)
    ring = lambda r: pltpu.VMEM((r, cw, lb), F32)
    return pl.pallas_call(
        functools.partial(_finish_shift_kernel, s_len=s_len, ff_chunk=ff_chunk),
        grid=(n // tm, n_sub),
        in_specs=[row(d), row(Q_A), row(Q_B), row(Q_X)] + [const(a) for a in (wo, g1, g2, wg, wu, wd, g3)]
                 + [unit] * 3 + [hbm] * 2 + [const(cnt)],
        out_specs=[row(d), unit, hbm, hbm],
        out_shape=[jax.ShapeDtypeStruct((n, d), F32), jax.ShapeDtypeStruct(qn.shape, F32),
                   jax.ShapeDtypeStruct(ck.shape, F32), jax.ShapeDtypeStruct(cv.shape, F32)],
        scratch_shapes=[pltpu.VMEM((tm, d), F32), pltpu.VMEM((tm, d), BF16), pltpu.VMEM((tm, d), F32),
                        pltpu.VMEM((tm, ff_chunk), BF16),
                        pltpu.VMEM((cw, lb + LANES), BF16), pltpu.VMEM((cw, lb + LANES), BF16),
                        ring(IN_RING), ring(IN_RING), ring(OUT_RING), ring(OUT_RING),
                        pltpu.SemaphoreType.DMA((2, IN_RING)), pltpu.SemaphoreType.DMA((2, OUT_RING))],
        compiler_params=_params(2),
        name="finish_shift",
    )(x, oa, ob, ox, wo, g1, g2, wg, wu, wd, g3, qn, kn, vn, ck, cv, cnt)


def _rope_tables(pos):
    half = HEAD_DIM // 2
    inv = ROPE_THETA ** (-jnp.arange(half, dtype=F32) / half)
    ang = pos.astype(F32)[:, None] * inv[None, :]
    cos = jnp.tile(jnp.cos(ang), (1, LANES // half))
    sin = jnp.tile(jnp.sin(ang), (1, LANES // half))
    first_half = (jnp.arange(LANES) % HEAD_DIM) < half
    return cos, jnp.where(first_half[None, :], -sin, sin)


def _swa_head_perm():
    return np.concatenate([np.arange(h * HEAD_DIM, (h + 1) * HEAD_DIM) for h in SWA_HEAD_ORDER])


def _seq_minor(c):
    b, n, h, dh = c.shape
    return jnp.transpose(c, (0, 2, 3, 1)).reshape(b, h * dh, n)


def _seq_major(a, heads):
    b, c, n = a.shape
    return jnp.transpose(a.reshape(b, heads, c // heads, n), (0, 3, 1, 2))


def kernel(x_prompt, x_sample, cache_swa_k, cache_swa_v, cache_dil_k, cache_dil_v, cache_mem_k, cache_mem_v,
           mem_prompt, g_pre_mix, w_in, sinks, w_mem_kv, w_o, g_post_mix, g_pre_ffn, w_gate, w_up, w_down,
           g_post_ffn):
    depth = w_in.shape[0]
    bp, t_p, d = x_prompt.shape
    bs, s_len, _ = x_sample.shape
    d_ff = w_gate.shape[2]
    perm = _swa_head_perm()
    cos_p, sin_p = _rope_tables(jnp.arange(t_p))
    cos_s, sin_s = _rope_tables(PAST_LEN + jnp.arange(s_len))
    tm_s = 256
    cos_s = jnp.tile(cos_s, (tm_s // s_len, 1))
    sin_s = jnp.tile(sin_s, (tm_s // s_len, 1))
    ff_chunk = 2 * LANES
    assert d_ff % ff_chunk == 0
    n_pairs = Q_B // LANES
    lb_a = cache_swa_k.shape[2]
    lb_b = cache_dil_k.shape[2]
    assert lb_a >= SWA_WINDOW and lb_b >= DIL_MAX_WINDOW, "every window position must lie inside the cache"
    assert lb_a % LANES == 0 and lb_b % LANES == 0 and s_len <= LANES
    amask, bcnt = (jnp.asarray(a) for a in _sample_tables(s_len, lb_a, lb_b, lb_a + LANES, lb_b + LANES))
    n_a = min(SWA_WINDOW, t_p)
    n_b = min(DIL_MAX_WINDOW, t_p)

    hp = x_prompt
    hs = x_sample.reshape(1, bs * s_len, d)
    outs = [[] for _ in range(10)]
    for l in range(depth):
        w_in_l = w_in[l]
        w_in_bf = jnp.concatenate([w_in_l[:, :Q_A][:, perm], w_in_l[:, Q_A:]], axis=1).astype(BF16)
        w_o_bf = jnp.concatenate([w_o[l][:Q_A][perm], w_o[l][Q_A:]], axis=0).astype(BF16)
        w_mem_bf = w_mem_kv[l].astype(BF16)
        wg, wu, wd = w_gate[l].astype(BF16), w_up[l].astype(BF16), w_down[l].astype(BF16)
        g0, g1, g2, g3 = (g[l][None, :] for g in (g_pre_mix, g_post_mix, g_pre_ffn, g_post_ffn))
        sink_heads = sinks[l].reshape(-1)[np.array(SWA_HEAD_ORDER)].astype(F32)
        sink_row = jnp.repeat(sink_heads, HEAD_DIM)[None, :]
        sink_rows = jnp.repeat(sink_heads, s_len)[:, None]

        qa, qb, qx, kah, vah, kbf, vbf, kat, vat, kbt, vbt = _proj(
            hp, g0, w_in_bf, cos_p, sin_p, tm=512, prompt=True, n_a=n_a, n_b=n_b)
        qa_s, qb_s, qx_s, kan, van, kbn, vbn = (a[0] for a in _proj(
            hs, g0, w_in_bf, cos_s, sin_s, tm=tm_s, prompt=False))

        mkt, mvt, mkh, mvh = _memkv(mem_prompt, w_mem_bf)
        oa, ox = _swa_mem(qa, kah, vah, qx, mkh, mvh, sink_row, qt=1024)
        ob = _dil(qb, kbf, vbf, qt=2048)

        to_units = lambda a: jnp.transpose(a.reshape(bs, s_len, n_pairs, LANES), (0, 2, 1, 3)).reshape(
            bs * n_pairs, s_len, LANES)
        cache_units = lambda c: _seq_minor(c).reshape(bs * n_pairs, LANES, lb_b)
        flat = lambda a: a.reshape(bp * t_p, a.shape[-1])
        hp, ob_s, sbk, sbv = _finish_shift(
            flat(hp), flat(oa), flat(ob), flat(ox), w_o_bf, g1, g2, wg, wu, wd, g3,
            to_units(qb_s), to_units(kbn), to_units(vbn), cache_units(cache_dil_k[l]), cache_units(cache_dil_v[l]),
            bcnt, tm=512, s_len=s_len, ff_chunk=ff_chunk)
        hp = hp.reshape(bp, t_p, d)
        ob_s = jnp.transpose(ob_s.reshape(bs, n_pairs, s_len, LANES), (0, 2, 1, 3)).reshape(bs * s_len, Q_B)
        outs[0].append(_seq_major(kat, SWA_KV_HEADS))
        outs[1].append(_seq_major(vat, SWA_KV_HEADS))
        outs[2].append(_seq_major(kbt, DIL_HEADS))
        outs[3].append(_seq_major(vbt, DIL_HEADS))
        outs[4].append(_seq_major(mkt, MEM_HEADS))
        outs[5].append(_seq_major(mvt, MEM_HEADS))

        oa_s, ox_s, sak, sav = _sample_small(
            qa_s, kan, van, qx_s, _seq_minor(cache_swa_k[l]), _seq_minor(cache_swa_v[l]),
            _seq_minor(cache_mem_k[l]), _seq_minor(cache_mem_v[l]), sink_rows, amask, s_len=s_len, group=8)
        hs = _finish(hs[0], oa_s, ob_s, ox_s, w_o_bf, g1, g2, wg, wu, wd, g3, tm=256, ff_chunk=ff_chunk)[None]
        outs[6].append(_seq_major(sak, SWA_KV_HEADS))
        outs[7].append(_seq_major(sav, SWA_KV_HEADS))
        outs[8].append(_seq_major(sbk.reshape(bs, KV_B, lb_b), DIL_HEADS))
        outs[9].append(_seq_major(sbv.reshape(bs, KV_B, lb_b), DIL_HEADS))

    return (hp, hs.reshape(bs, s_len, d)) + tuple(jnp.stack(o) for o in outs)
```

```python
import functools

import numpy as np
import jax
import jax.numpy as jnp
from jax import lax
from jax.experimental import pallas as pl
from jax.experimental.pallas import tpu as pltpu

F32 = jnp.float32
BF16 = jnp.bfloat16

HEAD_DIM = 64
SWA_KV_HEADS = 2
SWA_GROUP = 3
SWA_WINDOW = 128
DIL_HEADS = 6
DIL_PATTERNS = ((128, 1), (512, 4), (2048, 16))
DIL_MAX_WINDOW = 2048
MEM_HEADS = 4
BLK = 128
ROPE_THETA = 10000.0
EPS = 1e-6
SCALE = HEAD_DIM ** -0.5
PAST_LEN = 16384

Q_A = SWA_KV_HEADS * SWA_GROUP * HEAD_DIM
KV_A = SWA_KV_HEADS * HEAD_DIM
Q_B = DIL_HEADS * HEAD_DIM
KV_B = DIL_HEADS * HEAD_DIM
Q_X = MEM_HEADS * HEAD_DIM
IN_SIZES = (Q_A, KV_A, KV_A, Q_B, KV_B, KV_B, Q_X)
IN_OFFS = tuple(int(v) for v in np.cumsum((0,) + IN_SIZES))

LANES = 128
VMEM_LIMIT = 56 * 1024 * 1024

SWA_HEAD_ORDER = (0, 3, 1, 4, 2, 5)


def _params(n_axes):
    return pltpu.CompilerParams(dimension_semantics=("arbitrary",) * n_axes,
                                vmem_limit_bytes=VMEM_LIMIT)


def _rms(x, g):
    return x * lax.rsqrt(jnp.mean(x * x, axis=-1, keepdims=True) + EPS) * g


def _lane_lo(shape):
    return lax.broadcasted_iota(jnp.int32, shape, len(shape) - 1) % LANES < HEAD_DIM


def _proj_kernel(x_ref, g_ref, w_ref, cos_ref, sin_ref, *out_refs, prompt, n_a, n_b):
    tm = x_ref.shape[0]
    u = _rms(x_ref[...], g_ref[...]).astype(BF16)
    cos = cos_ref[...]
    sin = sin_ref[...]
    first_half = lax.broadcasted_iota(jnp.int32, cos.shape, 1) % HEAD_DIM < HEAD_DIM // 2

    z = jnp.dot(u, w_ref[...], preferred_element_type=F32)

    def proj(i):
        return z[:, IN_OFFS[i]:IN_OFFS[i + 1]]

    def rope(z):
        tiles = []
        for s in range(z.shape[1] // LANES):
            zs = z[:, s * LANES:(s + 1) * LANES]
            partner = jnp.where(first_half, pltpu.roll(zs, LANES - HEAD_DIM // 2, 1),
                                pltpu.roll(zs, HEAD_DIM // 2, 1))
            tiles.append(zs * cos + partner * sin)
        return jnp.concatenate(tiles, axis=1) if len(tiles) > 1 else tiles[0]

    qa = rope(proj(0))
    ka = rope(proj(1))
    va = proj(2)
    qb = rope(proj(3))
    kb = rope(proj(4))
    vb = proj(5)
    qx = proj(6)
    qa_ref, qb_ref, qx_ref, ka_ref, va_ref, kb_ref, vb_ref = out_refs[:7]
    qa_ref[...] = (qa * SCALE).astype(qa_ref.dtype)
    qx_ref[...] = (qx * SCALE).astype(qx_ref.dtype)
    ka_ref[...] = ka.astype(ka_ref.dtype)
    va_ref[...] = va.astype(va_ref.dtype)
    if not prompt:
        qb_ref[...] = qb * SCALE
        kb_ref[...] = kb
        vb_ref[...] = vb
    else:
        for p in range(Q_B // LANES):
            cols = slice(p * LANES, (p + 1) * LANES)
            qb_ref[p] = qb[:, cols] * SCALE
            kb_ref[p] = kb[:, cols]
            vb_ref[p] = vb[:, cols]
        kat_ref, vat_ref, kbt_ref, vbt_ref = out_refs[7:]
        j = pl.program_id(1)
        nj = pl.num_programs(1)

        @pl.when(j >= nj - n_b // tm)
        def _():
            kbt_ref[...] = kb.T
            vbt_ref[...] = vb.T

        @pl.when(j == nj - 1)
        def _():
            kat_ref[...] = ka[tm - n_a:, :].T
            vat_ref[...] = va[tm - n_a:, :].T


def _proj(x, g, w_bf, cos, sin, *, tm, prompt, n_a=0, n_b=0):
    b, t, d = x.shape
    nj = t // tm
    n_rope = cos.shape[0] // tm
    tile = lambda c: pl.BlockSpec((None, tm, c), lambda i, j: (i, j, 0))
    rope = pl.BlockSpec((tm, LANES), lambda i, j: (j % n_rope, 0))
    dt = BF16 if prompt else F32
    widths = (Q_A, Q_B, Q_X, KV_A, KV_A, KV_B, KV_B)
    out_specs = [tile(c) for c in widths]
    out_shape = [jax.ShapeDtypeStruct((b, t, c), dt) for c in widths]
    if prompt:
        assert n_a <= tm and n_a % LANES == 0 and n_b % tm == 0 and n_b <= t
        n_pairs = Q_B // LANES
        for i in (1, 5, 6):
            out_specs[i] = pl.BlockSpec((None, n_pairs, tm, LANES), lambda i, j: (i, 0, j, 0))
            out_shape[i] = jax.ShapeDtypeStruct((b, n_pairs, t, LANES), F32)
        first_b = nj - n_b // tm
        out_specs += [pl.BlockSpec((None, KV_A, n_a), lambda i, j: (i, 0, 0))] * 2
        out_specs += [pl.BlockSpec((None, KV_B, tm), lambda i, j: (i, 0, jnp.maximum(j - first_b, 0)))] * 2
        out_shape += [jax.ShapeDtypeStruct((b, KV_A, n_a), F32)] * 2 + [jax.ShapeDtypeStruct((b, KV_B, n_b), F32)] * 2
    return pl.pallas_call(
        functools.partial(_proj_kernel, prompt=prompt, n_a=n_a, n_b=n_b),
        grid=(b, nj),
        in_specs=[tile(d), pl.BlockSpec((1, d), lambda i, j: (0, 0)),
                  pl.BlockSpec(w_bf.shape, lambda i, j: (0, 0), pipeline_mode=pl.Buffered(1)),
                  rope, rope],
        out_specs=out_specs,
        out_shape=out_shape,
        compiler_params=_params(2),
        name="proj_prompt" if prompt else "proj_sample",
    )(x, g, w_bf, cos, sin)


def _memkv_kernel(mem_ref, w_ref, mkt_ref, mvt_ref, mkh_ref, mvh_ref):
    z = jnp.dot(mem_ref[...].astype(BF16), w_ref[...], preferred_element_type=F32)
    mk = z[:, :Q_X]
    mv = z[:, Q_X:]
    mkt_ref[...] = mk.T
    mvt_ref[...] = mv.T
    mkh_ref[...] = mk.astype(BF16)
    mvh_ref[...] = mv.astype(BF16)


def _memkv(mem, w_bf):
    b, m, d = mem.shape
    blk = lambda r, c: pl.BlockSpec((None, r, c), lambda i: (i, 0, 0))
    return pl.pallas_call(
        _memkv_kernel,
        grid=(b,),
        in_specs=[blk(m, d), pl.BlockSpec(w_bf.shape, lambda i: (0, 0))],
        out_specs=[blk(Q_X, m)] * 2 + [blk(m, Q_X)] * 2,
        out_shape=[jax.ShapeDtypeStruct((b, Q_X, m), F32)] * 2 + [jax.ShapeDtypeStruct((b, m, Q_X), BF16)] * 2,
        compiler_params=_params(1),
        name="memkv",
    )(mem, w_bf)


def _pair_attend(q, kk, vv, bias):
    tq = q.shape[0]
    zero = jnp.zeros_like(q)
    q_lo = _lane_lo(q.shape)
    q2 = jnp.concatenate([jnp.where(q_lo, q, zero), jnp.where(q_lo, zero, q)], axis=0)
    s = lax.dot_general(q2, kk, (((1,), (1,)), ((), ())), preferred_element_type=F32)
    stats = []
    for h in range(2):
        sh = s[h * tq:(h + 1) * tq]
        if bias is not None:
            sh = sh + bias
        m = jnp.max(sh, axis=1, keepdims=True)
        p = jnp.exp(sh - m)
        stats.append((m, jnp.sum(p, axis=1, keepdims=True), p.astype(BF16)))
    v_lo = _lane_lo(vv.shape)
    vzero = jnp.zeros_like(vv)
    o = (jnp.dot(stats[0][2], jnp.where(v_lo, vv, vzero), preferred_element_type=F32)
         + jnp.dot(stats[1][2], jnp.where(v_lo, vzero, vv), preferred_element_type=F32))
    o_lo = _lane_lo(o.shape)
    m = jnp.where(o_lo, stats[0][0], stats[1][0])
    l = jnp.where(o_lo, stats[0][1], stats[1][1])
    return o, m, l


def _band_bias():
    qi = np.arange(BLK)[:, None]
    c = np.arange(2 * BLK)[None, :]
    in_prev = (c < BLK) & (c >= qi)
    in_cur = (c >= BLK) & (c - BLK <= qi)
    allowed = np.stack([in_cur, in_prev | in_cur])
    return np.where(allowed, 0.0, -np.inf).astype(np.float32)


def _band_rows(j, i, n_blk):
    n = j * n_blk + i
    r0 = pl.multiple_of(i * BLK, BLK)
    k_cur = pl.multiple_of(n * BLK, BLK)
    k_prev = pl.multiple_of(jnp.maximum(n - 1, 0) * BLK, BLK)
    return n, r0, k_prev, k_cur


def _swa_mem_kernel(qa_ref, ka_ref, va_ref, qx_ref, mk_ref, mv_ref, sink_ref, bias_ref, oa_ref, ox_ref, *, qt):
    j = pl.program_id(1)
    n_blk = qt // BLK

    def body(i, carry):
        n, r0, k_prev, k_cur = _band_rows(j, i, n_blk)
        rows = pl.ds(r0, BLK)
        mask = bias_ref[jnp.minimum(n, 1)]
        kk = jnp.concatenate([ka_ref[pl.ds(k_prev, BLK), :], ka_ref[pl.ds(k_cur, BLK), :]], axis=0)
        vv = jnp.concatenate([va_ref[pl.ds(k_prev, BLK), :], va_ref[pl.ds(k_cur, BLK), :]], axis=0)
        for p in range(Q_A // LANES):
            cols = slice(p * LANES, (p + 1) * LANES)
            o, m, l = _pair_attend(qa_ref[rows, cols], kk, vv, mask)
            sink = sink_ref[:, cols]
            mm = jnp.maximum(m, sink)
            a = jnp.exp(m - mm)
            den = l * a + jnp.exp(sink - mm)
            oa_ref[rows, cols] = (o * (a / den)).astype(oa_ref.dtype)
        for p in range(Q_X // LANES):
            cols = slice(p * LANES, (p + 1) * LANES)
            o, m, l = _pair_attend(qx_ref[rows, cols], mk_ref[:, cols], mv_ref[:, cols], None)
            ox_ref[rows, cols] = (o / l).astype(ox_ref.dtype)
        return carry

    lax.fori_loop(0, n_blk, body, 0, unroll=2)


def _swa_mem(qa, ka, va, qx, mk, mv, sink_row, *, qt):
    b, t, _ = qa.shape
    m = mk.shape[1]
    qtile = lambda c: pl.BlockSpec((None, qt, c), lambda i, j: (i, j, 0))
    whole = lambda r, c: pl.BlockSpec((None, r, c), lambda i, j: (i, 0, 0))
    return pl.pallas_call(
        functools.partial(_swa_mem_kernel, qt=qt),
        grid=(b, t // qt),
        in_specs=[qtile(Q_A), whole(t, KV_A), whole(t, KV_A), qtile(Q_X), whole(m, Q_X), whole(m, Q_X),
                  pl.BlockSpec((1, Q_A), lambda i, j: (0, 0)),
                  pl.BlockSpec((2, BLK, 2 * BLK), lambda i, j: (0, 0, 0))],
        out_specs=[qtile(Q_A), qtile(Q_X)],
        out_shape=[jax.ShapeDtypeStruct((b, t, Q_A), BF16), jax.ShapeDtypeStruct((b, t, Q_X), BF16)],
        compiler_params=_params(2),
        name="swa_mem",
    )(qa, ka, va, qx, mk, mv, sink_row, jnp.asarray(_band_bias()))


def _dil_kernel(q_ref, k_ref, v_ref, bias_ref, ob_ref, *acc, qt):
    j = pl.program_id(1)
    n_pairs = Q_B // LANES
    patterns = sorted(DIL_PATTERNS, key=lambda wr: -wr[1])
    for idx, (w, r) in enumerate(patterns):
        assert w // r == BLK and qt % (r * BLK) == 0
        n_blk = qt // (r * BLK)

        def body(step, carry, r=r, n_blk=n_blk, first=(idx == 0)):
            a = lax.div(step, n_blk)
            i = lax.rem(step, n_blk)
            n = j * n_blk + i

            def rows(blk):
                if r == 1:
                    return pl.ds(pl.multiple_of(blk * BLK, BLK), BLK)
                return pl.ds(blk * (BLK * r) + a, BLK, stride=r)

            prev = jnp.maximum(n - 1, 0)
            bias = bias_ref[jnp.minimum(n, 1)]
            for p in range(n_pairs):
                acc_o, acc_m, acc_l = acc[3 * p:3 * p + 3]
                q = q_ref[p, rows(i), :].astype(BF16)
                kk = jnp.concatenate([k_ref[p, rows(prev), :], k_ref[p, rows(n), :]], axis=0).astype(BF16)
                vv = jnp.concatenate([v_ref[p, rows(prev), :], v_ref[p, rows(n), :]], axis=0).astype(BF16)
                o, m, l = _pair_attend(q, kk, vv, bias)
                if not first:
                    m_old = acc_m[rows(i), :]
                    m_new = jnp.maximum(m_old, m)
                    e_old = jnp.exp(m_old - m_new)
                    e = jnp.exp(m - m_new)
                    o = acc_o[rows(i), :] * e_old + o * e
                    l = acc_l[rows(i), :] * e_old + l * e
                    m = m_new
                acc_o[rows(i), :] = o
                acc_m[rows(i), :] = m
                acc_l[rows(i), :] = l
            return carry

        lax.fori_loop(0, r * n_blk, body, 0, unroll=2)
    for p in range(n_pairs):
        ob_ref[:, p * LANES:(p + 1) * LANES] = (acc[3 * p][...] / acc[3 * p + 2][...]).astype(ob_ref.dtype)


def _dil(q, k, v, *, qt):
    b, n_pairs, t, _ = q.shape
    return pl.pallas_call(
        functools.partial(_dil_kernel, qt=qt),
        grid=(b, t // qt),
        in_specs=[pl.BlockSpec((None, n_pairs, qt, LANES), lambda i, j: (i, 0, j, 0)),
                  pl.BlockSpec((None, n_pairs, t, LANES), lambda i, j: (i, 0, 0, 0)),
                  pl.BlockSpec((None, n_pairs, t, LANES), lambda i, j: (i, 0, 0, 0)),
                  pl.BlockSpec((2, BLK, 2 * BLK), lambda i, j: (0, 0, 0))],
        out_specs=pl.BlockSpec((None, qt, n_pairs * LANES), lambda i, j: (i, j, 0)),
        out_shape=jax.ShapeDtypeStruct((b, t, n_pairs * LANES), BF16),
        scratch_shapes=[pltpu.VMEM((qt, LANES), F32)] * (3 * n_pairs),
        compiler_params=_params(2),
        name="dil",
    )(q, k, v, jnp.asarray(_band_bias()))


def _shift_and_stage(cache_ref, new_ref, out_ref, stage, s_len):
    c, lb = cache_ref.shape
    new = new_ref[...]
    new_t = jnp.concatenate([new, jnp.zeros((LANES - s_len, c), F32)], axis=0).T
    cache = cache_ref[...]
    stage[:, 0:lb] = cache.astype(BF16)
    stage[:, lb:] = new_t.astype(BF16)
    shifted = pltpu.roll(cache, lb - s_len, 1)
    if lb > LANES:
        out_ref[:, 0:lb - LANES] = shifted[:, 0:lb - LANES]
    keep = lax.broadcasted_iota(jnp.int32, (c, LANES), 1) < LANES - s_len
    out_ref[:, lb - LANES:] = jnp.where(keep, shifted[:, lb - LANES:], pltpu.roll(new_t, LANES - s_len, 1))


def _scores_t(qblk, keys_t):
    return jnp.dot(qblk.astype(BF16), keys_t, preferred_element_type=F32)


def _weighted_t(p, values_t):
    return lax.dot_general(p.astype(BF16), values_t, (((1,), (1,)), ((), ())), preferred_element_type=F32)


def _sample_small_kernel(qa_ref, kan_ref, van_ref, qx_ref, cak_ref, cav_ref, cmk_ref, cmv_ref, sink_ref, amask_ref,
                         oa_ref, ox_ref, sak_ref, sav_ref, ka_s, va_s, *, s_len):
    lo = _lane_lo((s_len, LANES))
    for b in range(cak_ref.shape[0]):
        tok = slice(b * s_len, (b + 1) * s_len)
        _shift_and_stage(cak_ref.at[b], kan_ref.at[tok], sak_ref.at[b], ka_s.at[b], s_len)
        _shift_and_stage(cav_ref.at[b], van_ref.at[tok], sav_ref.at[b], va_s.at[b], s_len)

        qa = qa_ref[tok, :]
        pieces = []
        for p in range(Q_A // LANES):
            tile = qa[:, p * LANES:(p + 1) * LANES]
            pieces += [jnp.where(lo, tile, 0.0), jnp.where(lo, 0.0, tile)]
        s = _scores_t(jnp.concatenate(pieces, axis=0), ka_s[b])
        s = jnp.where(amask_ref[...] > 0.0, s, -jnp.inf)
        m = jnp.max(s, axis=1, keepdims=True)
        p_ = jnp.exp(s - m)
        l = jnp.sum(p_, axis=1, keepdims=True)
        o = _weighted_t(p_, va_s[b])
        sink = sink_ref[...]
        mm = jnp.maximum(m, sink)
        a = jnp.exp(m - mm)
        o = o * (a / (l * a + jnp.exp(sink - mm)))
        oa_ref[tok, :] = jnp.concatenate(
            [jnp.where(lo, o[2 * p * s_len:(2 * p + 1) * s_len], o[(2 * p + 1) * s_len:(2 * p + 2) * s_len])
             for p in range(Q_A // LANES)], axis=1)

        qx = jnp.concatenate([qx_ref[tok, :]] * MEM_HEADS, axis=0)
        row_h = lax.broadcasted_iota(jnp.int32, qx.shape, 0) // s_len
        col_h = lax.broadcasted_iota(jnp.int32, qx.shape, 1) // HEAD_DIM
        own = row_h == col_h
        s = _scores_t(jnp.where(own, qx, 0.0), cmk_ref[b].astype(BF16))
        m = jnp.max(s, axis=1, keepdims=True)
        p_ = jnp.exp(s - m)
        l = jnp.sum(p_, axis=1, keepdims=True)
        o_all = jnp.where(own, _weighted_t(p_, cmv_ref[b].astype(BF16)) / l, 0.0)
        o = o_all[0:s_len]
        for h in range(1, MEM_HEADS):
            o = o + o_all[h * s_len:(h + 1) * s_len]
        ox_ref[tok, :] = o


def _sample_tables(s_len, lb_a, lb_b, na, nb):
    i = np.arange(s_len)[:, None]
    dist_a = lb_a + i - np.arange(na)[None, :]
    amask = ((dist_a >= 0) & (dist_a <= SWA_WINDOW)).astype(np.float32)
    dist_b = lb_b + i - np.arange(nb)[None, :]
    cnt = np.zeros(dist_b.shape, np.float32)
    for w, r in DIL_PATTERNS:
        cnt += ((dist_b >= 0) & (dist_b <= w) & (dist_b % r == 0)).astype(np.float32)
    return np.tile(amask, (SWA_KV_HEADS * SWA_GROUP, 1)), np.tile(cnt, (2, 1))


def _sample_small(qa, kan, van, qx, cak, cav, cmk, cmv, sink_rows, amask, *, s_len, group):
    nbat, _, lb_a = cak.shape
    mtok = cmk.shape[2]
    na = amask.shape[1]
    assert nbat % group == 0
    tok = lambda c: pl.BlockSpec((group * s_len, c), lambda n: (n, 0))
    per_b = lambda r, c: pl.BlockSpec((group, r, c), lambda n: (n, 0, 0))
    const = lambda a: pl.BlockSpec(a.shape, lambda n: (0, 0))
    n_tok = nbat * s_len
    return pl.pallas_call(
        functools.partial(_sample_small_kernel, s_len=s_len),
        grid=(nbat // group,),
        in_specs=[tok(Q_A), tok(KV_A), tok(KV_A), tok(Q_X),
                  per_b(KV_A, lb_a), per_b(KV_A, lb_a), per_b(Q_X, mtok), per_b(Q_X, mtok),
                  const(sink_rows), const(amask)],
        out_specs=[tok(Q_A), tok(Q_X), per_b(KV_A, lb_a), per_b(KV_A, lb_a)],
        out_shape=[jax.ShapeDtypeStruct((n_tok, Q_A), F32), jax.ShapeDtypeStruct((n_tok, Q_X), F32),
                   jax.ShapeDtypeStruct(cak.shape, F32), jax.ShapeDtypeStruct(cav.shape, F32)],
        scratch_shapes=[pltpu.VMEM((group, KV_A, na), BF16), pltpu.VMEM((group, KV_A, na), BF16)],
        compiler_params=_params(1),
        name="sample_small",
    )(qa, kan, van, qx, cak, cav, cmk, cmv, sink_rows, amask)


def _new_positions_t(new_ref, s_len):
    c = new_ref.shape[1]
    return jnp.concatenate([new_ref[...], jnp.zeros((LANES - s_len, c), F32)], axis=0).T


def _shifted(cache, new_t, s_len):
    c, lb = cache.shape
    rolled = pltpu.roll(cache, lb - s_len, 1)
    keep = lax.broadcasted_iota(jnp.int32, (c, LANES), 1) < LANES - s_len
    tail = jnp.where(keep, rolled[:, lb - LANES:], pltpu.roll(new_t, LANES - s_len, 1))
    return jnp.concatenate([rolled[:, 0:lb - LANES], tail], axis=1) if lb > LANES else tail


def _dil_unit(qn_ref, kn_ref, vn_ref, ck_ref, cv_ref, cnt_ref, obs_ref, sk_ref, sv_ref, s_len):
    lb = ck_ref.shape[1]
    kc, vc = ck_ref[...], cv_ref[...]
    kn_t, vn_t = _new_positions_t(kn_ref, s_len), _new_positions_t(vn_ref, s_len)
    qn = qn_ref[...]
    lo = _lane_lo(qn.shape)
    qblk = jnp.concatenate([jnp.where(lo, qn, 0.0), jnp.where(lo, 0.0, qn)], axis=0)
    cnt_c, cnt_n = cnt_ref[:, 0:lb], cnt_ref[:, lb:]
    s_c = jnp.where(cnt_c > 0.0, _scores_t(qblk, kc.astype(BF16)), -jnp.inf)
    s_n = jnp.where(cnt_n > 0.0, _scores_t(qblk, kn_t.astype(BF16)), -jnp.inf)
    m = jnp.maximum(jnp.max(s_c, axis=1, keepdims=True), jnp.max(s_n, axis=1, keepdims=True))
    p_c = cnt_c * jnp.exp(s_c - m)
    p_n = cnt_n * jnp.exp(s_n - m)
    l = jnp.sum(p_c, axis=1, keepdims=True) + jnp.sum(p_n, axis=1, keepdims=True)
    o = (_weighted_t(p_c, vc.astype(BF16)) + _weighted_t(p_n, vn_t.astype(BF16))) / l
    obs_ref[...] = jnp.where(lo, o[0:s_len], o[s_len:2 * s_len])
    sk_ref[...] = _shifted(kc, kn_t, s_len)
    sv_ref[...] = _shifted(vc, vn_t, s_len)


def _mix_out(x_ref, oa_ref, ob_ref, ox_ref, wo_ref, g1_ref, g2_ref):
    cat = jnp.concatenate([oa_ref[...].astype(BF16), ob_ref[...].astype(BF16), ox_ref[...].astype(BF16)], axis=1)
    x = x_ref[...] + _rms(jnp.dot(cat, wo_ref[...], preferred_element_type=F32), g1_ref[...])
    return x, _rms(x, g2_ref[...]).astype(BF16)


def _gate_up(h, wg, wu):
    g = jnp.dot(h, wg, preferred_element_type=F32)
    u = jnp.dot(h, wu, preferred_element_type=F32)
    return (g * (1.0 / (1.0 + jnp.exp(-g))) * u).astype(BF16)


def _ffn_chunk(h, wg, wu, wd):
    return jnp.dot(_gate_up(h, wg, wu), wd, preferred_element_type=F32)


def _finish_kernel(x_ref, oa_ref, ob_ref, ox_ref, wo_ref, g1_ref, g2_ref, wg_ref, wu_ref, wd_ref, g3_ref, out_ref,
                   *, ff_chunk):
    x, h = _mix_out(x_ref, oa_ref, ob_ref, ox_ref, wo_ref, g1_ref, g2_ref)
    f = 0.0
    for c0 in range(0, wg_ref.shape[1], ff_chunk):
        cols = slice(c0, c0 + ff_chunk)
        f = f + _ffn_chunk(h, wg_ref[:, cols], wu_ref[:, cols], wd_ref[cols, :])
    out_ref[...] = x + _rms(f, g3_ref[...])


def _finish(x, oa, ob, ox, wo, g1, g2, wg, wu, wd, g3, *, tm, ff_chunk):
    n, d = x.shape
    row = lambda c: pl.BlockSpec((tm, c), lambda i: (i, 0))
    const = lambda a: pl.BlockSpec(a.shape, lambda i: (0,) * a.ndim, pipeline_mode=pl.Buffered(1))
    return pl.pallas_call(
        functools.partial(_finish_kernel, ff_chunk=ff_chunk),
        grid=(n // tm,),
        in_specs=[row(d), row(Q_A), row(Q_B), row(Q_X)] + [const(a) for a in (wo, g1, g2, wg, wu, wd, g3)],
        out_specs=row(d),
        out_shape=jax.ShapeDtypeStruct((n, d), F32),
        compiler_params=_params(1),
        name="finish",
    )(x, oa, ob, ox, wo, g1, g2, wg, wu, wd, g3)


def _finish_shift_kernel(x_ref, oa_ref, ob_ref, ox_ref, wo_ref, g1_ref, g2_ref, wg_ref, wu_ref, wd_ref, g3_ref,
                         qn_ref, kn_ref, vn_ref, ck_hbm, cv_hbm, cnt_ref,
                         out_ref, obs_ref, sk_hbm, sv_hbm,
                         x1_s, h_s, f_s, act_s, kin, vin, kout, vout, sem_in, sem_out, *, s_len, ff_chunk):
    c = pl.program_id(1)
    step = pl.program_id(0) * pl.num_programs(1) + c
    n_steps = pl.num_programs(0) * pl.num_programs(1)
    r_in, r_out = kin.shape[0], kout.shape[0]
    d_ff = wg_ref.shape[1]
    slot = lax.rem(step, r_in)
    oslot = lax.rem(step, r_out)

    def fetch(u, s):
        return (pltpu.make_async_copy(ck_hbm.at[u], kin.at[s], sem_in.at[0, s]),
                pltpu.make_async_copy(cv_hbm.at[u], vin.at[s], sem_in.at[1, s]))

    def flush(u, s):
        return (pltpu.make_async_copy(kout.at[s], sk_hbm.at[u], sem_out.at[0, s]),
                pltpu.make_async_copy(vout.at[s], sv_hbm.at[u], sem_out.at[1, s]))

    @pl.when(step == 0)
    def _():
        for u in range(r_in):
            for cp in fetch(u, u):
                cp.start()

    for cp in fetch(step, slot):
        cp.wait()

    @pl.when(step >= r_out)
    def _():
        for cp in flush(step - r_out, oslot):
            cp.wait()

    n_sub = d_ff // ff_chunk + 1
    static_slots = n_sub % r_in == 0 and n_sub % r_out == 0
    for cc in range(n_sub):
        @pl.when(c == cc)
        def _(cc=cc):
            s_in, s_out = (cc % r_in, cc % r_out) if static_slots else (slot, oslot)
            _dil_unit(qn_ref.at[0], kn_ref.at[0], vn_ref.at[0], kin.at[s_in], vin.at[s_in], cnt_ref,
                      obs_ref.at[0], kout.at[s_out], vout.at[s_out], s_len)
            prev = slice((cc - 1) * ff_chunk, cc * ff_chunk)
            cur = slice(cc * ff_chunk, (cc + 1) * ff_chunk)
            if cc == 0:
                x, h = _mix_out(x_ref, oa_ref, ob_ref, ox_ref, wo_ref, g1_ref, g2_ref)
                x1_s[...] = x
                h_s[...] = h
                f_s[...] = jnp.zeros_like(f_s)
                act_s[...] = _gate_up(h, wg_ref[:, cur], wu_ref[:, cur])
            elif cc < n_sub - 1:
                down = jnp.dot(act_s[...], wd_ref[prev, :], preferred_element_type=F32)
                act_s[...] = _gate_up(h_s[...], wg_ref[:, cur], wu_ref[:, cur])
                f_s[...] += down
            else:
                f = f_s[...] + jnp.dot(act_s[...], wd_ref[prev, :], preferred_element_type=F32)
                out_ref[...] = x1_s[...] + _rms(f, g3_ref[...])

    for cp in flush(step, oslot):
        cp.start()

    @pl.when(step + r_in < n_steps)
    def _():
        for cp in fetch(step + r_in, slot):
            cp.start()

    @pl.when(step == n_steps - 1)
    def _():
        for k in range(r_out):
            u = n_steps - 1 - k
            for cp in flush(u, lax.rem(u, r_out)):
                cp.wait()


IN_RING = 4
OUT_RING = 3


def _finish_shift(x, oa, ob, ox, wo, g1, g2, wg, wu, wd, g3, qn, kn, vn, ck, cv, cnt, *, tm, s_len, ff_chunk):
    n, d = x.shape
    n_sub = wg.shape[1] // ff_chunk + 1
    n_units, cw, lb = ck.shape
    assert n_units == (n // tm) * n_sub and n_units >= max(IN_RING, OUT_RING), "one cache unit per grid step"
    row = lambda c: pl.BlockSpec((tm, c), lambda i, c_: (i, 0))
    const = lambda a: pl.BlockSpec(a.shape, lambda i, c_: (0,) * a.ndim, pipeline_mode=pl.Buffered(1))
    unit = pl.BlockSpec((1, s_len, cw), lambda i, c_: (i * n_sub + c_, 0, 0))
    hbm = pl.BlockSpec(memory_space=pl.ANY)
    ring = lambda r: pltpu.VMEM((r, cw, lb), F32)
    return pl.pallas_call(
        functools.partial(_finish_shift_kernel, s_len=s_len, ff_chunk=ff_chunk),
        grid=(n // tm, n_sub),
        in_specs=[row(d), row(Q_A), row(Q_B), row(Q_X)] + [const(a) for a in (wo, g1, g2, wg, wu, wd, g3)]
                 + [unit] * 3 + [hbm] * 2 + [const(cnt)],
        out_specs=[row(d), unit, hbm, hbm],
        out_shape=[jax.ShapeDtypeStruct((n, d), F32), jax.ShapeDtypeStruct(qn.shape, F32),
                   jax.ShapeDtypeStruct(ck.shape, F32), jax.ShapeDtypeStruct(cv.shape, F32)],
        scratch_shapes=[pltpu.VMEM((tm, d), F32), pltpu.VMEM((tm, d), BF16), pltpu.VMEM((tm, d), F32),
                        pltpu.VMEM((tm, ff_chunk), BF16),
                        ring(IN_RING), ring(IN_RING), ring(OUT_RING), ring(OUT_RING),
                        pltpu.SemaphoreType.DMA((2, IN_RING)), pltpu.SemaphoreType.DMA((2, OUT_RING))],
        compiler_params=_params(2),
        name="finish_shift",
    )(x, oa, ob, ox, wo, g1, g2, wg, wu, wd, g3, qn, kn, vn, ck, cv, cnt)


def _rope_tables(pos):
    half = HEAD_DIM // 2
    inv = ROPE_THETA ** (-jnp.arange(half, dtype=F32) / half)
    ang = pos.astype(F32)[:, None] * inv[None, :]
    cos = jnp.tile(jnp.cos(ang), (1, LANES // half))
    sin = jnp.tile(jnp.sin(ang), (1, LANES // half))
    first_half = (jnp.arange(LANES) % HEAD_DIM) < half
    return cos, jnp.where(first_half[None, :], -sin, sin)


def _swa_head_perm():
    return np.concatenate([np.arange(h * HEAD_DIM, (h + 1) * HEAD_DIM) for h in SWA_HEAD_ORDER])


def _seq_minor(c):
    b, n, h, dh = c.shape
    return jnp.transpose(c, (0, 2, 3, 1)).reshape(b, h * dh, n)


def _seq_major(a, heads):
    b, c, n = a.shape
    return jnp.transpose(a.reshape(b, heads, c // heads, n), (0, 3, 1, 2))


def kernel(x_prompt, x_sample, cache_swa_k, cache_swa_v, cache_dil_k, cache_dil_v, cache_mem_k, cache_mem_v,
           mem_prompt, g_pre_mix, w_in, sinks, w_mem_kv, w_o, g_post_mix, g_pre_ffn, w_gate, w_up, w_down,
           g_post_ffn):
    depth = w_in.shape[0]
    bp, t_p, d = x_prompt.shape
    bs, s_len, _ = x_sample.shape
    d_ff = w_gate.shape[2]
    perm = _swa_head_perm()
    cos_p, sin_p = _rope_tables(jnp.arange(t_p))
    cos_s, sin_s = _rope_tables(PAST_LEN + jnp.arange(s_len))
    tm_s = 256
    cos_s = jnp.tile(cos_s, (tm_s // s_len, 1))
    sin_s = jnp.tile(sin_s, (tm_s // s_len, 1))
    ff_chunk = 2 * LANES
    assert d_ff % ff_chunk == 0
    n_pairs = Q_B // LANES
    lb_a = cache_swa_k.shape[2]
    lb_b = cache_dil_k.shape[2]
    assert lb_a >= SWA_WINDOW and lb_b >= DIL_MAX_WINDOW, "every window position must lie inside the cache"
    assert lb_a % LANES == 0 and lb_b % LANES == 0 and s_len <= LANES
    amask, bcnt = (jnp.asarray(a) for a in _sample_tables(s_len, lb_a, lb_b, lb_a + LANES, lb_b + LANES))
    n_a = min(SWA_WINDOW, t_p)
    n_b = min(DIL_MAX_WINDOW, t_p)

    hp = x_prompt
    hs = x_sample.reshape(1, bs * s_len, d)
    outs = [[] for _ in range(10)]
    for l in range(depth):
        w_in_l = w_in[l]
        w_in_bf = jnp.concatenate([w_in_l[:, :Q_A][:, perm], w_in_l[:, Q_A:]], axis=1).astype(BF16)
        w_o_bf = jnp.concatenate([w_o[l][:Q_A][perm], w_o[l][Q_A:]], axis=0).astype(BF16)
        w_mem_bf = w_mem_kv[l].astype(BF16)
        wg, wu, wd = w_gate[l].astype(BF16), w_up[l].astype(BF16), w_down[l].astype(BF16)
        g0, g1, g2, g3 = (g[l][None, :] for g in (g_pre_mix, g_post_mix, g_pre_ffn, g_post_ffn))
        sink_heads = sinks[l].reshape(-1)[np.array(SWA_HEAD_ORDER)].astype(F32)
        sink_row = jnp.repeat(sink_heads, HEAD_DIM)[None, :]
        sink_rows = jnp.repeat(sink_heads, s_len)[:, None]

        qa, qb, qx, kah, vah, kbf, vbf, kat, vat, kbt, vbt = _proj(
            hp, g0, w_in_bf, cos_p, sin_p, tm=512, prompt=True, n_a=n_a, n_b=n_b)
        qa_s, qb_s, qx_s, kan, van, kbn, vbn = (a[0] for a in _proj(
            hs, g0, w_in_bf, cos_s, sin_s, tm=tm_s, prompt=False))

        mkt, mvt, mkh, mvh = _memkv(mem_prompt, w_mem_bf)
        oa, ox = _swa_mem(qa, kah, vah, qx, mkh, mvh, sink_row, qt=1024)
        ob = _dil(qb, kbf, vbf, qt=2048)

        to_units = lambda a: jnp.transpose(a.reshape(bs, s_len, n_pairs, LANES), (0, 2, 1, 3)).reshape(
            bs * n_pairs, s_len, LANES)
        cache_units = lambda c: _seq_minor(c).reshape(bs * n_pairs, LANES, lb_b)
        flat = lambda a: a.reshape(bp * t_p, a.shape[-1])
        hp, ob_s, sbk, sbv = _finish_shift(
            flat(hp), flat(oa), flat(ob), flat(ox), w_o_bf, g1, g2, wg, wu, wd, g3,
            to_units(qb_s), to_units(kbn), to_units(vbn), cache_units(cache_dil_k[l]), cache_units(cache_dil_v[l]),
            bcnt, tm=512, s_len=s_len, ff_chunk=ff_chunk)
        hp = hp.reshape(bp, t_p, d)
        ob_s = jnp.transpose(ob_s.reshape(bs, n_pairs, s_len, LANES), (0, 2, 1, 3)).reshape(bs * s_len, Q_B)
        outs[0].append(_seq_major(kat, SWA_KV_HEADS))
        outs[1].append(_seq_major(vat, SWA_KV_HEADS))
        outs[2].append(_seq_major(kbt, DIL_HEADS))
        outs[3].append(_seq_major(vbt, DIL_HEADS))
        outs[4].append(_seq_major(mkt, MEM_HEADS))
        outs[5].append(_seq_major(mvt, MEM_HEADS))

        oa_s, ox_s, sak, sav = _sample_small(
            qa_s, kan, van, qx_s, _seq_minor(cache_swa_k[l]), _seq_minor(cache_swa_v[l]),
            _seq_minor(cache_mem_k[l]), _seq_minor(cache_mem_v[l]), sink_rows, amask, s_len=s_len, group=8)
        hs = _finish(hs[0], oa_s, ob_s, ox_s, w_o_bf, g1, g2, wg, wu, wd, g3, tm=256, ff_chunk=ff_chunk)[None]
        outs[6].append(_seq_major(sak, SWA_KV_HEADS))
        outs[7].append(_seq_major(sav, SWA_KV_HEADS))
        outs[8].append(_seq_major(sbk.reshape(bs, KV_B, lb_b), DIL_HEADS))
        outs[9].append(_seq_major(sbv.reshape(bs, KV_B, lb_b), DIL_HEADS))

    return (hp, hs.reshape(bs, s_len, d)) + tuple(jnp.stack(o) for o in outs)
```

```python
import functools

import numpy as np
import jax
import jax.numpy as jnp
from jax import lax
from jax.experimental import pallas as pl
from jax.experimental.pallas import tpu as pltpu

F32 = jnp.float32
BF16 = jnp.bfloat16

HEAD_DIM = 64
SWA_KV_HEADS = 2
SWA_GROUP = 3
SWA_WINDOW = 128
DIL_HEADS = 6
DIL_PATTERNS = ((128, 1), (512, 4), (2048, 16))
DIL_MAX_WINDOW = 2048
MEM_HEADS = 4
BLK = 128
ROPE_THETA = 10000.0
EPS = 1e-6
SCALE = HEAD_DIM ** -0.5
PAST_LEN = 16384

Q_A = SWA_KV_HEADS * SWA_GROUP * HEAD_DIM
KV_A = SWA_KV_HEADS * HEAD_DIM
Q_B = DIL_HEADS * HEAD_DIM
KV_B = DIL_HEADS * HEAD_DIM
Q_X = MEM_HEADS * HEAD_DIM
IN_SIZES = (Q_A, KV_A, KV_A, Q_B, KV_B, KV_B, Q_X)
IN_OFFS = tuple(int(v) for v in np.cumsum((0,) + IN_SIZES))

LANES = 128
VMEM_LIMIT = 56 * 1024 * 1024

SWA_HEAD_ORDER = (0, 3, 1, 4, 2, 5)


def _params(n_axes):
    return pltpu.CompilerParams(dimension_semantics=("arbitrary",) * n_axes,
                                vmem_limit_bytes=VMEM_LIMIT)


def _rms(x, g):
    return x * lax.rsqrt(jnp.mean(x * x, axis=-1, keepdims=True) + EPS) * g


def _lane_lo(shape):
    return lax.broadcasted_iota(jnp.int32, shape, len(shape) - 1) % LANES < HEAD_DIM


def _proj_kernel(x_ref, g_ref, w_ref, cos_ref, sin_ref, *out_refs, prompt, n_a, n_b):
    tm = x_ref.shape[0]
    u = _rms(x_ref[...], g_ref[...]).astype(BF16)
    cos = cos_ref[...]
    sin = sin_ref[...]
    first_half = lax.broadcasted_iota(jnp.int32, cos.shape, 1) % HEAD_DIM < HEAD_DIM // 2

    z = jnp.dot(u, w_ref[...], preferred_element_type=F32)

    def proj(i):
        return z[:, IN_OFFS[i]:IN_OFFS[i + 1]]

    def rope(z):
        tiles = []
        for s in range(z.shape[1] // LANES):
            zs = z[:, s * LANES:(s + 1) * LANES]
            partner = jnp.where(first_half, pltpu.roll(zs, LANES - HEAD_DIM // 2, 1),
                                pltpu.roll(zs, HEAD_DIM // 2, 1))
            tiles.append(zs * cos + partner * sin)
        return jnp.concatenate(tiles, axis=1) if len(tiles) > 1 else tiles[0]

    qa = rope(proj(0))
    ka = rope(proj(1))
    va = proj(2)
    qb = rope(proj(3))
    kb = rope(proj(4))
    vb = proj(5)
    qx = proj(6)
    qa_ref, qb_ref, qx_ref, ka_ref, va_ref, kb_ref, vb_ref = out_refs[:7]
    qa_ref[...] = (qa * SCALE).astype(qa_ref.dtype)
    qx_ref[...] = (qx * SCALE).astype(qx_ref.dtype)
    ka_ref[...] = ka.astype(ka_ref.dtype)
    va_ref[...] = va.astype(va_ref.dtype)
    if not prompt:
        qb_ref[...] = qb * SCALE
        kb_ref[...] = kb
        vb_ref[...] = vb
    else:
        for p in range(Q_B // LANES):
            cols = slice(p * LANES, (p + 1) * LANES)
            qb_ref[p] = qb[:, cols] * SCALE
            kb_ref[p] = kb[:, cols]
            vb_ref[p] = vb[:, cols]
        kat_ref, vat_ref, kbt_ref, vbt_ref = out_refs[7:]
        j = pl.program_id(1)
        nj = pl.num_programs(1)

        @pl.when(j >= nj - n_b // tm)
        def _():
            kbt_ref[...] = kb.T
            vbt_ref[...] = vb.T

        @pl.when(j == nj - 1)
        def _():
            kat_ref[...] = ka[tm - n_a:, :].T
            vat_ref[...] = va[tm - n_a:, :].T


def _proj(x, g, w_bf, cos, sin, *, tm, prompt, n_a=0, n_b=0):
    b, t, d = x.shape
    nj = t // tm
    n_rope = cos.shape[0] // tm
    tile = lambda c: pl.BlockSpec((None, tm, c), lambda i, j: (i, j, 0))
    rope = pl.BlockSpec((tm, LANES), lambda i, j: (j % n_rope, 0))
    dt = BF16 if prompt else F32
    widths = (Q_A, Q_B, Q_X, KV_A, KV_A, KV_B, KV_B)
    out_specs = [tile(c) for c in widths]
    out_shape = [jax.ShapeDtypeStruct((b, t, c), dt) for c in widths]
    if prompt:
        assert n_a <= tm and n_a % LANES == 0 and n_b % tm == 0 and n_b <= t
        n_pairs = Q_B // LANES
        for i in (1, 5, 6):
            out_specs[i] = pl.BlockSpec((None, n_pairs, tm, LANES), lambda i, j: (i, 0, j, 0))
            out_shape[i] = jax.ShapeDtypeStruct((b, n_pairs, t, LANES), F32)
        first_b = nj - n_b // tm
        out_specs += [pl.BlockSpec((None, KV_A, n_a), lambda i, j: (i, 0, 0))] * 2
        out_specs += [pl.BlockSpec((None, KV_B, tm), lambda i, j: (i, 0, jnp.maximum(j - first_b, 0)))] * 2
        out_shape += [jax.ShapeDtypeStruct((b, KV_A, n_a), F32)] * 2 + [jax.ShapeDtypeStruct((b, KV_B, n_b), F32)] * 2
    return pl.pallas_call(
        functools.partial(_proj_kernel, prompt=prompt, n_a=n_a, n_b=n_b),
        grid=(b, nj),
        in_specs=[tile(d), pl.BlockSpec((1, d), lambda i, j: (0, 0)),
                  pl.BlockSpec(w_bf.shape, lambda i, j: (0, 0), pipeline_mode=pl.Buffered(1)),
                  rope, rope],
        out_specs=out_specs,
        out_shape=out_shape,
        compiler_params=_params(2),
        name="proj_prompt" if prompt else "proj_sample",
    )(x, g, w_bf, cos, sin)


def _memkv_kernel(mem_ref, w_ref, mkt_ref, mvt_ref, mkh_ref, mvh_ref):
    z = jnp.dot(mem_ref[...].astype(BF16), w_ref[...], preferred_element_type=F32)
    mk = z[:, :Q_X]
    mv = z[:, Q_X:]
    mkt_ref[...] = mk.T
    mvt_ref[...] = mv.T
    mkh_ref[...] = mk.astype(BF16)
    mvh_ref[...] = mv.astype(BF16)


def _memkv(mem, w_bf):
    b, m, d = mem.shape
    blk = lambda r, c: pl.BlockSpec((None, r, c), lambda i: (i, 0, 0))
    return pl.pallas_call(
        _memkv_kernel,
        grid=(b,),
        in_specs=[blk(m, d), pl.BlockSpec(w_bf.shape, lambda i: (0, 0))],
        out_specs=[blk(Q_X, m)] * 2 + [blk(m, Q_X)] * 2,
        out_shape=[jax.ShapeDtypeStruct((b, Q_X, m), F32)] * 2 + [jax.ShapeDtypeStruct((b, m, Q_X), BF16)] * 2,
        compiler_params=_params(1),
        name="memkv",
    )(mem, w_bf)


def _pair_attend(q, kk, vv, bias):
    tq = q.shape[0]
    zero = jnp.zeros_like(q)
    q_lo = _lane_lo(q.shape)
    q2 = jnp.concatenate([jnp.where(q_lo, q, zero), jnp.where(q_lo, zero, q)], axis=0)
    s = lax.dot_general(q2, kk, (((1,), (1,)), ((), ())), preferred_element_type=F32)
    stats = []
    for h in range(2):
        sh = s[h * tq:(h + 1) * tq]
        if bias is not None:
            sh = sh + bias
        m = jnp.max(sh, axis=1, keepdims=True)
        p = jnp.exp(sh - m)
        stats.append((m, jnp.sum(p, axis=1, keepdims=True), p.astype(BF16)))
    v_lo = _lane_lo(vv.shape)
    vzero = jnp.zeros_like(vv)
    o = (jnp.dot(stats[0][2], jnp.where(v_lo, vv, vzero), preferred_element_type=F32)
         + jnp.dot(stats[1][2], jnp.where(v_lo, vzero, vv), preferred_element_type=F32))
    o_lo = _lane_lo(o.shape)
    m = jnp.where(o_lo, stats[0][0], stats[1][0])
    l = jnp.where(o_lo, stats[0][1], stats[1][1])
    return o, m, l


def _band_bias():
    qi = np.arange(BLK)[:, None]
    c = np.arange(2 * BLK)[None, :]
    in_prev = (c < BLK) & (c >= qi)
    in_cur = (c >= BLK) & (c - BLK <= qi)
    allowed = np.stack([in_cur, in_prev | in_cur])
    return np.where(allowed, 0.0, -np.inf).astype(np.float32)


def _band_rows(j, i, n_blk):
    n = j * n_blk + i
    r0 = pl.multiple_of(i * BLK, BLK)
    k_cur = pl.multiple_of(n * BLK, BLK)
    k_prev = pl.multiple_of(jnp.maximum(n - 1, 0) * BLK, BLK)
    return n, r0, k_prev, k_cur


def _swa_mem_kernel(qa_ref, ka_ref, va_ref, qx_ref, mk_ref, mv_ref, sink_ref, bias_ref, oa_ref, ox_ref, *, qt):
    j = pl.program_id(1)
    n_blk = qt // BLK

    def body(i, carry):
        n, r0, k_prev, k_cur = _band_rows(j, i, n_blk)
        rows = pl.ds(r0, BLK)
        mask = bias_ref[jnp.minimum(n, 1)]
        kk = jnp.concatenate([ka_ref[pl.ds(k_prev, BLK), :], ka_ref[pl.ds(k_cur, BLK), :]], axis=0)
        vv = jnp.concatenate([va_ref[pl.ds(k_prev, BLK), :], va_ref[pl.ds(k_cur, BLK), :]], axis=0)
        for p in range(Q_A // LANES):
            cols = slice(p * LANES, (p + 1) * LANES)
            o, m, l = _pair_attend(qa_ref[rows, cols], kk, vv, mask)
            sink = sink_ref[:, cols]
            mm = jnp.maximum(m, sink)
            a = jnp.exp(m - mm)
            den = l * a + jnp.exp(sink - mm)
            oa_ref[rows, cols] = (o * (a / den)).astype(oa_ref.dtype)
        for p in range(Q_X // LANES):
            cols = slice(p * LANES, (p + 1) * LANES)
            o, m, l = _pair_attend(qx_ref[rows, cols], mk_ref[:, cols], mv_ref[:, cols], None)
            ox_ref[rows, cols] = (o / l).astype(ox_ref.dtype)
        return carry

    lax.fori_loop(0, n_blk, body, 0, unroll=2)


def _swa_mem(qa, ka, va, qx, mk, mv, sink_row, *, qt):
    b, t, _ = qa.shape
    m = mk.shape[1]
    qtile = lambda c: pl.BlockSpec((None, qt, c), lambda i, j: (i, j, 0))
    whole = lambda r, c: pl.BlockSpec((None, r, c), lambda i, j: (i, 0, 0))
    return pl.pallas_call(
        functools.partial(_swa_mem_kernel, qt=qt),
        grid=(b, t // qt),
        in_specs=[qtile(Q_A), whole(t, KV_A), whole(t, KV_A), qtile(Q_X), whole(m, Q_X), whole(m, Q_X),
                  pl.BlockSpec((1, Q_A), lambda i, j: (0, 0)),
                  pl.BlockSpec((2, BLK, 2 * BLK), lambda i, j: (0, 0, 0))],
        out_specs=[qtile(Q_A), qtile(Q_X)],
        out_shape=[jax.ShapeDtypeStruct((b, t, Q_A), BF16), jax.ShapeDtypeStruct((b, t, Q_X), BF16)],
        compiler_params=_params(2),
        name="swa_mem",
    )(qa, ka, va, qx, mk, mv, sink_row, jnp.asarray(_band_bias()))


def _dil_kernel(q_ref, k_ref, v_ref, bias_ref, ob_ref, *acc, qt):
    j = pl.program_id(1)
    n_pairs = Q_B // LANES
    patterns = sorted(DIL_PATTERNS, key=lambda wr: -wr[1])
    for idx, (w, r) in enumerate(patterns):
        assert w // r == BLK and qt % (r * BLK) == 0
        n_blk = qt // (r * BLK)

        def body(step, carry, r=r, n_blk=n_blk, first=(idx == 0)):
            a = lax.div(step, n_blk)
            i = lax.rem(step, n_blk)
            n = j * n_blk + i

            def rows(blk):
                if r == 1:
                    return pl.ds(pl.multiple_of(blk * BLK, BLK), BLK)
                return pl.ds(blk * (BLK * r) + a, BLK, stride=r)

            prev = jnp.maximum(n - 1, 0)
            bias = bias_ref[jnp.minimum(n, 1)]
            for p in range(n_pairs):
                acc_o, acc_m, acc_l = acc[3 * p:3 * p + 3]
                q = q_ref[p, rows(i), :].astype(BF16)
                kk = jnp.concatenate([k_ref[p, rows(prev), :], k_ref[p, rows(n), :]], axis=0).astype(BF16)
                vv = jnp.concatenate([v_ref[p, rows(prev), :], v_ref[p, rows(n), :]], axis=0).astype(BF16)
                o, m, l = _pair_attend(q, kk, vv, bias)
                if not first:
                    m_old = acc_m[rows(i), :]
                    m_new = jnp.maximum(m_old, m)
                    e_old = jnp.exp(m_old - m_new)
                    e = jnp.exp(m - m_new)
                    o = acc_o[rows(i), :] * e_old + o * e
                    l = acc_l[rows(i), :] * e_old + l * e
                    m = m_new
                acc_o[rows(i), :] = o
                acc_m[rows(i), :] = m
                acc_l[rows(i), :] = l
            return carry

        lax.fori_loop(0, r * n_blk, body, 0, unroll=2)
    for p in range(n_pairs):
        ob_ref[:, p * LANES:(p + 1) * LANES] = (acc[3 * p][...] / acc[3 * p + 2][...]).astype(ob_ref.dtype)


def _dil(q, k, v, *, qt):
    b, n_pairs, t, _ = q.shape
    return pl.pallas_call(
        functools.partial(_dil_kernel, qt=qt),
        grid=(b, t // qt),
        in_specs=[pl.BlockSpec((None, n_pairs, qt, LANES), lambda i, j: (i, 0, j, 0)),
                  pl.BlockSpec((None, n_pairs, t, LANES), lambda i, j: (i, 0, 0, 0)),
                  pl.BlockSpec((None, n_pairs, t, LANES), lambda i, j: (i, 0, 0, 0)),
                  pl.BlockSpec((2, BLK, 2 * BLK), lambda i, j: (0, 0, 0))],
        out_specs=pl.BlockSpec((None, qt, n_pairs * LANES), lambda i, j: (i, j, 0)),
        out_shape=jax.ShapeDtypeStruct((b, t, n_pairs * LANES), BF16),
        scratch_shapes=[pltpu.VMEM((qt, LANES), F32)] * (3 * n_pairs),
        compiler_params=_params(2),
        name="dil",
    )(q, k, v, jnp.asarray(_band_bias()))


def _shift_and_stage(cache_ref, new_ref, out_ref, stage, s_len):
    c, lb = cache_ref.shape
    new = new_ref[...]
    new_t = jnp.concatenate([new, jnp.zeros((LANES - s_len, c), F32)], axis=0).T
    cache = cache_ref[...]
    stage[:, 0:lb] = cache.astype(BF16)
    stage[:, lb:] = new_t.astype(BF16)
    shifted = pltpu.roll(cache, lb - s_len, 1)
    if lb > LANES:
        out_ref[:, 0:lb - LANES] = shifted[:, 0:lb - LANES]
    keep = lax.broadcasted_iota(jnp.int32, (c, LANES), 1) < LANES - s_len
    out_ref[:, lb - LANES:] = jnp.where(keep, shifted[:, lb - LANES:], pltpu.roll(new_t, LANES - s_len, 1))


def _scores_t(qblk, keys_t):
    return jnp.dot(qblk.astype(BF16), keys_t, preferred_element_type=F32)


def _weighted_t(p, values_t):
    return lax.dot_general(p.astype(BF16), values_t, (((1,), (1,)), ((), ())), preferred_element_type=F32)


def _sample_small_kernel(qa_ref, kan_ref, van_ref, qx_ref, cak_ref, cav_ref, cmk_ref, cmv_ref, sink_ref, amask_ref,
                         oa_ref, ox_ref, sak_ref, sav_ref, ka_s, va_s, *, s_len):
    lo = _lane_lo((s_len, LANES))
    for b in range(cak_ref.shape[0]):
        tok = slice(b * s_len, (b + 1) * s_len)
        _shift_and_stage(cak_ref.at[b], kan_ref.at[tok], sak_ref.at[b], ka_s.at[b], s_len)
        _shift_and_stage(cav_ref.at[b], van_ref.at[tok], sav_ref.at[b], va_s.at[b], s_len)

        qa = qa_ref[tok, :]
        pieces = []
        for p in range(Q_A // LANES):
            tile = qa[:, p * LANES:(p + 1) * LANES]
            pieces += [jnp.where(lo, tile, 0.0), jnp.where(lo, 0.0, tile)]
        s = _scores_t(jnp.concatenate(pieces, axis=0), ka_s[b])
        s = jnp.where(amask_ref[...] > 0.0, s, -jnp.inf)
        m = jnp.max(s, axis=1, keepdims=True)
        p_ = jnp.exp(s - m)
        l = jnp.sum(p_, axis=1, keepdims=True)
        o = _weighted_t(p_, va_s[b])
        sink = sink_ref[...]
        mm = jnp.maximum(m, sink)
        a = jnp.exp(m - mm)
        o = o * (a / (l * a + jnp.exp(sink - mm)))
        oa_ref[tok, :] = jnp.concatenate(
            [jnp.where(lo, o[2 * p * s_len:(2 * p + 1) * s_len], o[(2 * p + 1) * s_len:(2 * p + 2) * s_len])
             for p in range(Q_A // LANES)], axis=1)

        qx = jnp.concatenate([qx_ref[tok, :]] * MEM_HEADS, axis=0)
        row_h = lax.broadcasted_iota(jnp.int32, qx.shape, 0) // s_len
        col_h = lax.broadcasted_iota(jnp.int32, qx.shape, 1) // HEAD_DIM
        own = row_h == col_h
        s = _scores_t(jnp.where(own, qx, 0.0), cmk_ref[b].astype(BF16))
        m = jnp.max(s, axis=1, keepdims=True)
        p_ = jnp.exp(s - m)
        l = jnp.sum(p_, axis=1, keepdims=True)
        o_all = jnp.where(own, _weighted_t(p_, cmv_ref[b].astype(BF16)) / l, 0.0)
        o = o_all[0:s_len]
        for h in range(1, MEM_HEADS):
            o = o + o_all[h * s_len:(h + 1) * s_len]
        ox_ref[tok, :] = o


def _sample_tables(s_len, lb_a, lb_b, na, nb):
    i = np.arange(s_len)[:, None]
    dist_a = lb_a + i - np.arange(na)[None, :]
    amask = ((dist_a >= 0) & (dist_a <= SWA_WINDOW)).astype(np.float32)
    dist_b = lb_b + i - np.arange(nb)[None, :]
    cnt = np.zeros(dist_b.shape, np.float32)
    for w, r in DIL_PATTERNS:
        cnt += ((dist_b >= 0) & (dist_b <= w) & (dist_b % r == 0)).astype(np.float32)
    return np.tile(amask, (SWA_KV_HEADS * SWA_GROUP, 1)), np.tile(cnt, (2, 1))


def _sample_small(qa, kan, van, qx, cak, cav, cmk, cmv, sink_rows, amask, *, s_len, group):
    nbat, _, lb_a = cak.shape
    mtok = cmk.shape[2]
    na = amask.shape[1]
    assert nbat % group == 0
    tok = lambda c: pl.BlockSpec((group * s_len, c), lambda n: (n, 0))
    per_b = lambda r, c: pl.BlockSpec((group, r, c), lambda n: (n, 0, 0))
    const = lambda a: pl.BlockSpec(a.shape, lambda n: (0, 0))
    n_tok = nbat * s_len
    return pl.pallas_call(
        functools.partial(_sample_small_kernel, s_len=s_len),
        grid=(nbat // group,),
        in_specs=[tok(Q_A), tok(KV_A), tok(KV_A), tok(Q_X),
                  per_b(KV_A, lb_a), per_b(KV_A, lb_a), per_b(Q_X, mtok), per_b(Q_X, mtok),
                  const(sink_rows), const(amask)],
        out_specs=[tok(Q_A), tok(Q_X), per_b(KV_A, lb_a), per_b(KV_A, lb_a)],
        out_shape=[jax.ShapeDtypeStruct((n_tok, Q_A), F32), jax.ShapeDtypeStruct((n_tok, Q_X), F32),
                   jax.ShapeDtypeStruct(cak.shape, F32), jax.ShapeDtypeStruct(cav.shape, F32)],
        scratch_shapes=[pltpu.VMEM((group, KV_A, na), BF16), pltpu.VMEM((group, KV_A, na), BF16)],
        compiler_params=_params(1),
        name="sample_small",
    )(qa, kan, van, qx, cak, cav, cmk, cmv, sink_rows, amask)


def _new_positions_t(new_ref, s_len):
    c = new_ref.shape[1]
    return jnp.concatenate([new_ref[...], jnp.zeros((LANES - s_len, c), F32)], axis=0).T


def _shifted(cache, new_t, s_len):
    c, lb = cache.shape
    rolled = pltpu.roll(cache, lb - s_len, 1)
    keep = lax.broadcasted_iota(jnp.int32, (c, LANES), 1) < LANES - s_len
    tail = jnp.where(keep, rolled[:, lb - LANES:], pltpu.roll(new_t, LANES - s_len, 1))
    return jnp.concatenate([rolled[:, 0:lb - LANES], tail], axis=1) if lb > LANES else tail


def _dil_unit(qn_ref, kn_ref, vn_ref, ck_ref, cv_ref, cnt_ref, obs_ref, sk_ref, sv_ref, s_len):
    lb = ck_ref.shape[1]
    kc, vc = ck_ref[...], cv_ref[...]
    kn_t, vn_t = _new_positions_t(kn_ref, s_len), _new_positions_t(vn_ref, s_len)
    qn = qn_ref[...]
    lo = _lane_lo(qn.shape)
    qblk = jnp.concatenate([jnp.where(lo, qn, 0.0), jnp.where(lo, 0.0, qn)], axis=0)
    cnt_c, cnt_n = cnt_ref[:, 0:lb], cnt_ref[:, lb:]
    s_c = jnp.where(cnt_c > 0.0, _scores_t(qblk, kc.astype(BF16)), -jnp.inf)
    s_n = jnp.where(cnt_n > 0.0, _scores_t(qblk, kn_t.astype(BF16)), -jnp.inf)
    m = jnp.maximum(jnp.max(s_c, axis=1, keepdims=True), jnp.max(s_n, axis=1, keepdims=True))
    p_c = cnt_c * jnp.exp(s_c - m)
    p_n = cnt_n * jnp.exp(s_n - m)
    l = jnp.sum(p_c, axis=1, keepdims=True) + jnp.sum(p_n, axis=1, keepdims=True)
    o = (_weighted_t(p_c, vc.astype(BF16)) + _weighted_t(p_n, vn_t.astype(BF16))) / l
    obs_ref[...] = jnp.where(lo, o[0:s_len], o[s_len:2 * s_len])
    sk_ref[...] = _shifted(kc, kn_t, s_len)
    sv_ref[...] = _shifted(vc, vn_t, s_len)


def _mix_out(x_ref, oa_ref, ob_ref, ox_ref, wo_ref, g1_ref, g2_ref):
    cat = jnp.concatenate([oa_ref[...].astype(BF16), ob_ref[...].astype(BF16), ox_ref[...].astype(BF16)], axis=1)
    x = x_ref[...] + _rms(jnp.dot(cat, wo_ref[...], preferred_element_type=F32), g1_ref[...])
    return x, _rms(x, g2_ref[...]).astype(BF16)


def _gate_up(h, wg, wu):
    g = jnp.dot(h, wg, preferred_element_type=F32)
    u = jnp.dot(h, wu, preferred_element_type=F32)
    return (g * (1.0 / (1.0 + jnp.exp(-g))) * u).astype(BF16)


def _ffn_chunk(h, wg, wu, wd):
    return jnp.dot(_gate_up(h, wg, wu), wd, preferred_element_type=F32)


def _finish_kernel(x_ref, oa_ref, ob_ref, ox_ref, wo_ref, g1_ref, g2_ref, wg_ref, wu_ref, wd_ref, g3_ref, out_ref,
                   *, ff_chunk):
    x, h = _mix_out(x_ref, oa_ref, ob_ref, ox_ref, wo_ref, g1_ref, g2_ref)
    f = 0.0
    for c0 in range(0, wg_ref.shape[1], ff_chunk):
        cols = slice(c0, c0 + ff_chunk)
        f = f + _ffn_chunk(h, wg_ref[:, cols], wu_ref[:, cols], wd_ref[cols, :])
    out_ref[...] = x + _rms(f, g3_ref[...])


def _finish(x, oa, ob, ox, wo, g1, g2, wg, wu, wd, g3, *, tm, ff_chunk):
    n, d = x.shape
    row = lambda c: pl.BlockSpec((tm, c), lambda i: (i, 0))
    const = lambda a: pl.BlockSpec(a.shape, lambda i: (0,) * a.ndim, pipeline_mode=pl.Buffered(1))
    return pl.pallas_call(
        functools.partial(_finish_kernel, ff_chunk=ff_chunk),
        grid=(n // tm,),
        in_specs=[row(d), row(Q_A), row(Q_B), row(Q_X)] + [const(a) for a in (wo, g1, g2, wg, wu, wd, g3)],
        out_specs=row(d),
        out_shape=jax.ShapeDtypeStruct((n, d), F32),
        compiler_params=_params(1),
        name="finish",
    )(x, oa, ob, ox, wo, g1, g2, wg, wu, wd, g3)


def _finish_shift_kernel(x_ref, oa_ref, ob_ref, ox_ref, wo_ref, g1_ref, g2_ref, wg_ref, wu_ref, wd_ref, g3_ref,
                         qn_ref, kn_ref, vn_ref, ck_hbm, cv_hbm, cnt_ref,
                         out_ref, obs_ref, sk_hbm, sv_hbm,
                         x1_s, h_s, f_s, act_s, kin, vin, kout, vout, sem_in, sem_out, *, s_len, ff_chunk):
    c = pl.program_id(1)
    step = pl.program_id(0) * pl.num_programs(1) + c
    n_steps = pl.num_programs(0) * pl.num_programs(1)
    r_in, r_out = kin.shape[0], kout.shape[0]
    d_ff = wg_ref.shape[1]
    slot = lax.rem(step, r_in)
    oslot = lax.rem(step, r_out)

    def fetch(u, s):
        return (pltpu.make_async_copy(ck_hbm.at[u], kin.at[s], sem_in.at[0, s]),
                pltpu.make_async_copy(cv_hbm.at[u], vin.at[s], sem_in.at[1, s]))

    def flush(u, s):
        return (pltpu.make_async_copy(kout.at[s], sk_hbm.at[u], sem_out.at[0, s]),
                pltpu.make_async_copy(vout.at[s], sv_hbm.at[u], sem_out.at[1, s]))

    @pl.when(step == 0)
    def _():
        for u in range(r_in):
            for cp in fetch(u, u):
                cp.start()

    for cp in fetch(step, slot):
        cp.wait()

    @pl.when(step >= r_out)
    def _():
        for cp in flush(step - r_out, oslot):
            cp.wait()

    n_sub = d_ff // ff_chunk + 1
    static_slots = n_sub % r_in == 0 and n_sub % r_out == 0
    for cc in range(n_sub):
        @pl.when(c == cc)
        def _(cc=cc):
            s_in, s_out = (cc % r_in, cc % r_out) if static_slots else (slot, oslot)
            _dil_unit(qn_ref.at[step], kn_ref.at[step], vn_ref.at[step], kin.at[s_in], vin.at[s_in], cnt_ref,
                      obs_ref.at[step], kout.at[s_out], vout.at[s_out], s_len)
            prev = slice((cc - 1) * ff_chunk, cc * ff_chunk)
            cur = slice(cc * ff_chunk, (cc + 1) * ff_chunk)
            if cc == 0:
                x, h = _mix_out(x_ref, oa_ref, ob_ref, ox_ref, wo_ref, g1_ref, g2_ref)
                x1_s[...] = x
                h_s[...] = h
                f_s[...] = jnp.zeros_like(f_s)
                act_s[...] = _gate_up(h, wg_ref[:, cur], wu_ref[:, cur])
            elif cc < n_sub - 1:
                down = jnp.dot(act_s[...], wd_ref[prev, :], preferred_element_type=F32)
                act_s[...] = _gate_up(h_s[...], wg_ref[:, cur], wu_ref[:, cur])
                f_s[...] += down
            else:
                f = f_s[...] + jnp.dot(act_s[...], wd_ref[prev, :], preferred_element_type=F32)
                out_ref[...] = x1_s[...] + _rms(f, g3_ref[...])

    for cp in flush(step, oslot):
        cp.start()

    @pl.when(step + r_in < n_steps)
    def _():
        for cp in fetch(step + r_in, slot):
            cp.start()

    @pl.when(step == n_steps - 1)
    def _():
        for k in range(r_out):
            u = n_steps - 1 - k
            for cp in flush(u, lax.rem(u, r_out)):
                cp.wait()


IN_RING = 3
OUT_RING = 2


def _finish_shift(x, oa, ob, ox, wo, g1, g2, wg, wu, wd, g3, qn, kn, vn, ck, cv, cnt, *, tm, s_len, ff_chunk):
    n, d = x.shape
    n_sub = wg.shape[1] // ff_chunk + 1
    n_units, cw, lb = ck.shape
    assert n_units == (n // tm) * n_sub and n_units >= max(IN_RING, OUT_RING), "one cache unit per grid step"
    row = lambda c: pl.BlockSpec((tm, c), lambda i, c_: (i, 0))
    const = lambda a: pl.BlockSpec(a.shape, lambda i, c_: (0,) * a.ndim, pipeline_mode=pl.Buffered(1))
    whole_out = pl.BlockSpec(qn.shape, lambda i, c_: (0, 0, 0))
    hbm = pl.BlockSpec(memory_space=pl.ANY)
    ring = lambda r: pltpu.VMEM((r, cw, lb), F32)
    return pl.pallas_call(
        functools.partial(_finish_shift_kernel, s_len=s_len, ff_chunk=ff_chunk),
        grid=(n // tm, n_sub),
        in_specs=[row(d), row(Q_A), row(Q_B), row(Q_X)] + [const(a) for a in (wo, g1, g2, wg, wu, wd, g3)]
                 + [const(a) for a in (qn, kn, vn)] + [hbm] * 2 + [const(cnt)],
        out_specs=[row(d), whole_out, hbm, hbm],
        out_shape=[jax.ShapeDtypeStruct((n, d), F32), jax.ShapeDtypeStruct(qn.shape, F32),
                   jax.ShapeDtypeStruct(ck.shape, F32), jax.ShapeDtypeStruct(cv.shape, F32)],
        scratch_shapes=[pltpu.VMEM((tm, d), F32), pltpu.VMEM((tm, d), BF16), pltpu.VMEM((tm, d), F32),
                        pltpu.VMEM((tm, ff_chunk), BF16),
                        ring(IN_RING), ring(IN_RING), ring(OUT_RING), ring(OUT_RING),
                        pltpu.SemaphoreType.DMA((2, IN_RING)), pltpu.SemaphoreType.DMA((2, OUT_RING))],
        compiler_params=_params(2),
        name="finish_shift",
    )(x, oa, ob, ox, wo, g1, g2, wg, wu, wd, g3, qn, kn, vn, ck, cv, cnt)


def _rope_tables(pos):
    half = HEAD_DIM // 2
    inv = ROPE_THETA ** (-jnp.arange(half, dtype=F32) / half)
    ang = pos.astype(F32)[:, None] * inv[None, :]
    cos = jnp.tile(jnp.cos(ang), (1, LANES // half))
    sin = jnp.tile(jnp.sin(ang), (1, LANES // half))
    first_half = (jnp.arange(LANES) % HEAD_DIM) < half
    return cos, jnp.where(first_half[None, :], -sin, sin)


def _swa_head_perm():
    return np.concatenate([np.arange(h * HEAD_DIM, (h + 1) * HEAD_DIM) for h in SWA_HEAD_ORDER])


def _seq_minor(c):
    b, n, h, dh = c.shape
    return jnp.transpose(c, (0, 2, 3, 1)).reshape(b, h * dh, n)


def _seq_major(a, heads):
    b, c, n = a.shape
    return jnp.transpose(a.reshape(b, heads, c // heads, n), (0, 3, 1, 2))


def kernel(x_prompt, x_sample, cache_swa_k, cache_swa_v, cache_dil_k, cache_dil_v, cache_mem_k, cache_mem_v,
           mem_prompt, g_pre_mix, w_in, sinks, w_mem_kv, w_o, g_post_mix, g_pre_ffn, w_gate, w_up, w_down,
           g_post_ffn):
    depth = w_in.shape[0]
    bp, t_p, d = x_prompt.shape
    bs, s_len, _ = x_sample.shape
    d_ff = w_gate.shape[2]
    perm = _swa_head_perm()
    cos_p, sin_p = _rope_tables(jnp.arange(t_p))
    cos_s, sin_s = _rope_tables(PAST_LEN + jnp.arange(s_len))
    tm_s = 256
    cos_s = jnp.tile(cos_s, (tm_s // s_len, 1))
    sin_s = jnp.tile(sin_s, (tm_s // s_len, 1))
    ff_chunk = 2 * LANES
    assert d_ff % ff_chunk == 0
    n_pairs = Q_B // LANES
    lb_a = cache_swa_k.shape[2]
    lb_b = cache_dil_k.shape[2]
    assert lb_a >= SWA_WINDOW and lb_b >= DIL_MAX_WINDOW, "every window position must lie inside the cache"
    assert lb_a % LANES == 0 and lb_b % LANES == 0 and s_len <= LANES
    amask, bcnt = (jnp.asarray(a) for a in _sample_tables(s_len, lb_a, lb_b, lb_a + LANES, lb_b + LANES))
    n_a = min(SWA_WINDOW, t_p)
    n_b = min(DIL_MAX_WINDOW, t_p)

    hp = x_prompt
    hs = x_sample.reshape(1, bs * s_len, d)
    outs = [[] for _ in range(10)]
    for l in range(depth):
        w_in_l = w_in[l]
        w_in_bf = jnp.concatenate([w_in_l[:, :Q_A][:, perm], w_in_l[:, Q_A:]], axis=1).astype(BF16)
        w_o_bf = jnp.concatenate([w_o[l][:Q_A][perm], w_o[l][Q_A:]], axis=0).astype(BF16)
        w_mem_bf = w_mem_kv[l].astype(BF16)
        wg, wu, wd = w_gate[l].astype(BF16), w_up[l].astype(BF16), w_down[l].astype(BF16)
        g0, g1, g2, g3 = (g[l][None, :] for g in (g_pre_mix, g_post_mix, g_pre_ffn, g_post_ffn))
        sink_heads = sinks[l].reshape(-1)[np.array(SWA_HEAD_ORDER)].astype(F32)
        sink_row = jnp.repeat(sink_heads, HEAD_DIM)[None, :]
        sink_rows = jnp.repeat(sink_heads, s_len)[:, None]

        qa, qb, qx, kah, vah, kbf, vbf, kat, vat, kbt, vbt = _proj(
            hp, g0, w_in_bf, cos_p, sin_p, tm=512, prompt=True, n_a=n_a, n_b=n_b)
        qa_s, qb_s, qx_s, kan, van, kbn, vbn = (a[0] for a in _proj(
            hs, g0, w_in_bf, cos_s, sin_s, tm=tm_s, prompt=False))

        mkt, mvt, mkh, mvh = _memkv(mem_prompt, w_mem_bf)
        oa, ox = _swa_mem(qa, kah, vah, qx, mkh, mvh, sink_row, qt=1024)
        ob = _dil(qb, kbf, vbf, qt=2048)

        to_units = lambda a: jnp.transpose(a.reshape(bs, s_len, n_pairs, LANES), (0, 2, 1, 3)).reshape(
            bs * n_pairs, s_len, LANES)
        cache_units = lambda c: _seq_minor(c).reshape(bs * n_pairs, LANES, lb_b)
        flat = lambda a: a.reshape(bp * t_p, a.shape[-1])
        hp, ob_s, sbk, sbv = _finish_shift(
            flat(hp), flat(oa), flat(ob), flat(ox), w_o_bf, g1, g2, wg, wu, wd, g3,
            to_units(qb_s), to_units(kbn), to_units(vbn), cache_units(cache_dil_k[l]), cache_units(cache_dil_v[l]),
            bcnt, tm=512, s_len=s_len, ff_chunk=ff_chunk)
        hp = hp.reshape(bp, t_p, d)
        ob_s = jnp.transpose(ob_s.reshape(bs, n_pairs, s_len, LANES), (0, 2, 1, 3)).reshape(bs * s_len, Q_B)
        outs[0].append(_seq_major(kat, SWA_KV_HEADS))
        outs[1].append(_seq_major(vat, SWA_KV_HEADS))
        outs[2].append(_seq_major(kbt, DIL_HEADS))
        outs[3].append(_seq_major(vbt, DIL_HEADS))
        outs[4].append(_seq_major(mkt, MEM_HEADS))
        outs[5].append(_seq_major(mvt, MEM_HEADS))

        oa_s, ox_s, sak, sav = _sample_small(
            qa_s, kan, van, qx_s, _seq_minor(cache_swa_k[l]), _seq_minor(cache_swa_v[l]),
            _seq_minor(cache_mem_k[l]), _seq_minor(cache_mem_v[l]), sink_rows, amask, s_len=s_len, group=8)
        hs = _finish(hs[0], oa_s, ob_s, ox_s, w_o_bf, g1, g2, wg, wu, wd, g3, tm=256, ff_chunk=ff_chunk)[None]
        outs[6].append(_seq_major(sak, SWA_KV_HEADS))
        outs[7].append(_seq_major(sav, SWA_KV_HEADS))
        outs[8].append(_seq_major(sbk.reshape(bs, KV_B, lb_b), DIL_HEADS))
        outs[9].append(_seq_major(sbv.reshape(bs, KV_B, lb_b), DIL_HEADS))

    return (hp, hs.reshape(bs, s_len, d)) + tuple(jnp.stack(o) for o in outs)
```

```python
import functools

import numpy as np
import jax
import jax.numpy as jnp
from jax import lax
from jax.experimental import pallas as pl
from jax.experimental.pallas import tpu as pltpu

F32 = jnp.float32
BF16 = jnp.bfloat16

HEAD_DIM = 64
SWA_KV_HEADS = 2
SWA_GROUP = 3
SWA_WINDOW = 128
DIL_HEADS = 6
DIL_PATTERNS = ((128, 1), (512, 4), (2048, 16))
DIL_MAX_WINDOW = 2048
MEM_HEADS = 4
BLK = 128
ROPE_THETA = 10000.0
EPS = 1e-6
SCALE = HEAD_DIM ** -0.5
PAST_LEN = 16384

Q_A = SWA_KV_HEADS * SWA_GROUP * HEAD_DIM
KV_A = SWA_KV_HEADS * HEAD_DIM
Q_B = DIL_HEADS * HEAD_DIM
KV_B = DIL_HEADS * HEAD_DIM
Q_X = MEM_HEADS * HEAD_DIM
IN_SIZES = (Q_A, KV_A, KV_A, Q_B, KV_B, KV_B, Q_X)
IN_OFFS = tuple(int(v) for v in np.cumsum((0,) + IN_SIZES))

LANES = 128
VMEM_LIMIT = 56 * 1024 * 1024

SWA_HEAD_ORDER = (0, 3, 1, 4, 2, 5)


def _params(n_axes):
    return pltpu.CompilerParams(dimension_semantics=("arbitrary",) * n_axes,
                                vmem_limit_bytes=VMEM_LIMIT)


def _rms(x, g):
    return x * lax.rsqrt(jnp.mean(x * x, axis=-1, keepdims=True) + EPS) * g


def _lane_lo(shape):
    return lax.broadcasted_iota(jnp.int32, shape, len(shape) - 1) % LANES < HEAD_DIM


def _proj_kernel(x_ref, g_ref, w_ref, cos_ref, sin_ref, *out_refs, prompt, n_a, n_b):
    tm = x_ref.shape[0]
    u = _rms(x_ref[...], g_ref[...]).astype(BF16)
    cos = cos_ref[...]
    sin = sin_ref[...]
    first_half = lax.broadcasted_iota(jnp.int32, cos.shape, 1) % HEAD_DIM < HEAD_DIM // 2

    z = jnp.dot(u, w_ref[...], preferred_element_type=F32)

    def proj(i):
        return z[:, IN_OFFS[i]:IN_OFFS[i + 1]]

    def rope(z):
        tiles = []
        for s in range(z.shape[1] // LANES):
            zs = z[:, s * LANES:(s + 1) * LANES]
            partner = jnp.where(first_half, pltpu.roll(zs, LANES - HEAD_DIM // 2, 1),
                                pltpu.roll(zs, HEAD_DIM // 2, 1))
            tiles.append(zs * cos + partner * sin)
        return jnp.concatenate(tiles, axis=1) if len(tiles) > 1 else tiles[0]

    qa = rope(proj(0))
    ka = rope(proj(1))
    va = proj(2)
    qb = rope(proj(3))
    kb = rope(proj(4))
    vb = proj(5)
    qx = proj(6)
    qa_ref, qb_ref, qx_ref, ka_ref, va_ref, kb_ref, vb_ref = out_refs[:7]
    qa_ref[...] = (qa * SCALE).astype(qa_ref.dtype)
    qx_ref[...] = (qx * SCALE).astype(qx_ref.dtype)
    ka_ref[...] = ka.astype(ka_ref.dtype)
    va_ref[...] = va.astype(va_ref.dtype)
    if not prompt:
        qb_ref[...] = qb * SCALE
        kb_ref[...] = kb
        vb_ref[...] = vb
    else:
        for p in range(Q_B // LANES):
            cols = slice(p * LANES, (p + 1) * LANES)
            qb_ref[p] = qb[:, cols] * SCALE
            kb_ref[p] = kb[:, cols]
            vb_ref[p] = vb[:, cols]
        kat_ref, vat_ref, kbt_ref, vbt_ref = out_refs[7:]
        j = pl.program_id(1)
        nj = pl.num_programs(1)

        @pl.when(j >= nj - n_b // tm)
        def _():
            kbt_ref[...] = kb.T
            vbt_ref[...] = vb.T

        @pl.when(j == nj - 1)
        def _():
            kat_ref[...] = ka[tm - n_a:, :].T
            vat_ref[...] = va[tm - n_a:, :].T


def _proj(x, g, w_bf, cos, sin, *, tm, prompt, n_a=0, n_b=0):
    b, t, d = x.shape
    nj = t // tm
    n_rope = cos.shape[0] // tm
    tile = lambda c: pl.BlockSpec((None, tm, c), lambda i, j: (i, j, 0))
    rope = pl.BlockSpec((tm, LANES), lambda i, j: (j % n_rope, 0))
    dt = BF16 if prompt else F32
    widths = (Q_A, Q_B, Q_X, KV_A, KV_A, KV_B, KV_B)
    out_specs = [tile(c) for c in widths]
    out_shape = [jax.ShapeDtypeStruct((b, t, c), dt) for c in widths]
    if prompt:
        assert n_a <= tm and n_a % LANES == 0 and n_b % tm == 0 and n_b <= t
        n_pairs = Q_B // LANES
        for i in (1, 5, 6):
            out_specs[i] = pl.BlockSpec((None, n_pairs, tm, LANES), lambda i, j: (i, 0, j, 0))
            out_shape[i] = jax.ShapeDtypeStruct((b, n_pairs, t, LANES), F32)
        first_b = nj - n_b // tm
        out_specs += [pl.BlockSpec((None, KV_A, n_a), lambda i, j: (i, 0, 0))] * 2
        out_specs += [pl.BlockSpec((None, KV_B, tm), lambda i, j: (i, 0, jnp.maximum(j - first_b, 0)))] * 2
        out_shape += [jax.ShapeDtypeStruct((b, KV_A, n_a), F32)] * 2 + [jax.ShapeDtypeStruct((b, KV_B, n_b), F32)] * 2
    return pl.pallas_call(
        functools.partial(_proj_kernel, prompt=prompt, n_a=n_a, n_b=n_b),
        grid=(b, nj),
        in_specs=[tile(d), pl.BlockSpec((1, d), lambda i, j: (0, 0)),
                  pl.BlockSpec(w_bf.shape, lambda i, j: (0, 0), pipeline_mode=pl.Buffered(1)),
                  rope, rope],
        out_specs=out_specs,
        out_shape=out_shape,
        compiler_params=_params(2),
        name="proj_prompt" if prompt else "proj_sample",
    )(x, g, w_bf, cos, sin)


def _memkv_kernel(mem_ref, w_ref, mkt_ref, mvt_ref, mkh_ref, mvh_ref):
    z = jnp.dot(mem_ref[...].astype(BF16), w_ref[...], preferred_element_type=F32)
    mk = z[:, :Q_X]
    mv = z[:, Q_X:]
    mkt_ref[...] = mk.T
    mvt_ref[...] = mv.T
    mkh_ref[...] = mk.astype(BF16)
    mvh_ref[...] = mv.astype(BF16)


def _memkv(mem, w_bf):
    b, m, d = mem.shape
    blk = lambda r, c: pl.BlockSpec((None, r, c), lambda i: (i, 0, 0))
    return pl.pallas_call(
        _memkv_kernel,
        grid=(b,),
        in_specs=[blk(m, d), pl.BlockSpec(w_bf.shape, lambda i: (0, 0))],
        out_specs=[blk(Q_X, m)] * 2 + [blk(m, Q_X)] * 2,
        out_shape=[jax.ShapeDtypeStruct((b, Q_X, m), F32)] * 2 + [jax.ShapeDtypeStruct((b, m, Q_X), BF16)] * 2,
        compiler_params=_params(1),
        name="memkv",
    )(mem, w_bf)


def _pair_attend(q, kk, vv, bias):
    tq = q.shape[0]
    zero = jnp.zeros_like(q)
    q_lo = _lane_lo(q.shape)
    q2 = jnp.concatenate([jnp.where(q_lo, q, zero), jnp.where(q_lo, zero, q)], axis=0)
    s = lax.dot_general(q2, kk, (((1,), (1,)), ((), ())), preferred_element_type=F32)
    stats = []
    for h in range(2):
        sh = s[h * tq:(h + 1) * tq]
        if bias is not None:
            sh = sh + bias
        m = jnp.max(sh, axis=1, keepdims=True)
        p = jnp.exp(sh - m)
        stats.append((m, jnp.sum(p, axis=1, keepdims=True), p.astype(BF16)))
    v_lo = _lane_lo(vv.shape)
    vzero = jnp.zeros_like(vv)
    o = (jnp.dot(stats[0][2], jnp.where(v_lo, vv, vzero), preferred_element_type=F32)
         + jnp.dot(stats[1][2], jnp.where(v_lo, vzero, vv), preferred_element_type=F32))
    o_lo = _lane_lo(o.shape)
    m = jnp.where(o_lo, stats[0][0], stats[1][0])
    l = jnp.where(o_lo, stats[0][1], stats[1][1])
    return o, m, l


def _band_bias():
    qi = np.arange(BLK)[:, None]
    c = np.arange(2 * BLK)[None, :]
    in_prev = (c < BLK) & (c >= qi)
    in_cur = (c >= BLK) & (c - BLK <= qi)
    allowed = np.stack([in_cur, in_prev | in_cur])
    return np.where(allowed, 0.0, -np.inf).astype(np.float32)


def _band_rows(j, i, n_blk):
    n = j * n_blk + i
    r0 = pl.multiple_of(i * BLK, BLK)
    k_cur = pl.multiple_of(n * BLK, BLK)
    k_prev = pl.multiple_of(jnp.maximum(n - 1, 0) * BLK, BLK)
    return n, r0, k_prev, k_cur


def _swa_mem_kernel(qa_ref, ka_ref, va_ref, qx_ref, mk_ref, mv_ref, sink_ref, bias_ref, oa_ref, ox_ref, *, qt):
    j = pl.program_id(1)
    n_blk = qt // BLK

    def body(i, carry):
        n, r0, k_prev, k_cur = _band_rows(j, i, n_blk)
        rows = pl.ds(r0, BLK)
        mask = bias_ref[jnp.minimum(n, 1)]
        kk = jnp.concatenate([ka_ref[pl.ds(k_prev, BLK), :], ka_ref[pl.ds(k_cur, BLK), :]], axis=0)
        vv = jnp.concatenate([va_ref[pl.ds(k_prev, BLK), :], va_ref[pl.ds(k_cur, BLK), :]], axis=0)
        for p in range(Q_A // LANES):
            cols = slice(p * LANES, (p + 1) * LANES)
            o, m, l = _pair_attend(qa_ref[rows, cols], kk, vv, mask)
            sink = sink_ref[:, cols]
            mm = jnp.maximum(m, sink)
            a = jnp.exp(m - mm)
            den = l * a + jnp.exp(sink - mm)
            oa_ref[rows, cols] = (o * (a / den)).astype(oa_ref.dtype)
        for p in range(Q_X // LANES):
            cols = slice(p * LANES, (p + 1) * LANES)
            o, m, l = _pair_attend(qx_ref[rows, cols], mk_ref[:, cols], mv_ref[:, cols], None)
            ox_ref[rows, cols] = (o / l).astype(ox_ref.dtype)
        return carry

    lax.fori_loop(0, n_blk, body, 0, unroll=2)


def _swa_mem(qa, ka, va, qx, mk, mv, sink_row, *, qt):
    b, t, _ = qa.shape
    m = mk.shape[1]
    qtile = lambda c: pl.BlockSpec((None, qt, c), lambda i, j: (i, j, 0))
    whole = lambda r, c: pl.BlockSpec((None, r, c), lambda i, j: (i, 0, 0))
    return pl.pallas_call(
        functools.partial(_swa_mem_kernel, qt=qt),
        grid=(b, t // qt),
        in_specs=[qtile(Q_A), whole(t, KV_A), whole(t, KV_A), qtile(Q_X), whole(m, Q_X), whole(m, Q_X),
                  pl.BlockSpec((1, Q_A), lambda i, j: (0, 0)),
                  pl.BlockSpec((2, BLK, 2 * BLK), lambda i, j: (0, 0, 0))],
        out_specs=[qtile(Q_A), qtile(Q_X)],
        out_shape=[jax.ShapeDtypeStruct((b, t, Q_A), BF16), jax.ShapeDtypeStruct((b, t, Q_X), BF16)],
        compiler_params=_params(2),
        name="swa_mem",
    )(qa, ka, va, qx, mk, mv, sink_row, jnp.asarray(_band_bias()))


def _dil_kernel(q_ref, k_ref, v_ref, bias_ref, ob_ref, *acc, qt):
    j = pl.program_id(1)
    n_pairs = Q_B // LANES
    patterns = sorted(DIL_PATTERNS, key=lambda wr: -wr[1])
    for idx, (w, r) in enumerate(patterns):
        assert w // r == BLK and qt % (r * BLK) == 0
        n_blk = qt // (r * BLK)

        def body(step, carry, r=r, n_blk=n_blk, first=(idx == 0)):
            a = lax.div(step, n_blk)
            i = lax.rem(step, n_blk)
            n = j * n_blk + i

            def rows(blk):
                if r == 1:
                    return pl.ds(pl.multiple_of(blk * BLK, BLK), BLK)
                return pl.ds(blk * (BLK * r) + a, BLK, stride=r)

            prev = jnp.maximum(n - 1, 0)
            bias = bias_ref[jnp.minimum(n, 1)]
            for p in range(n_pairs):
                acc_o, acc_m, acc_l = acc[3 * p:3 * p + 3]
                q = q_ref[p, rows(i), :].astype(BF16)
                kk = jnp.concatenate([k_ref[p, rows(prev), :], k_ref[p, rows(n), :]], axis=0).astype(BF16)
                vv = jnp.concatenate([v_ref[p, rows(prev), :], v_ref[p, rows(n), :]], axis=0).astype(BF16)
                o, m, l = _pair_attend(q, kk, vv, bias)
                if not first:
                    m_old = acc_m[rows(i), :]
                    m_new = jnp.maximum(m_old, m)
                    e_old = jnp.exp(m_old - m_new)
                    e = jnp.exp(m - m_new)
                    o = acc_o[rows(i), :] * e_old + o * e
                    l = acc_l[rows(i), :] * e_old + l * e
                    m = m_new
                acc_o[rows(i), :] = o
                acc_m[rows(i), :] = m
                acc_l[rows(i), :] = l
            return carry

        lax.fori_loop(0, r * n_blk, body, 0, unroll=2)
    for p in range(n_pairs):
        ob_ref[:, p * LANES:(p + 1) * LANES] = (acc[3 * p][...] / acc[3 * p + 2][...]).astype(ob_ref.dtype)


def _dil(q, k, v, *, qt):
    b, n_pairs, t, _ = q.shape
    return pl.pallas_call(
        functools.partial(_dil_kernel, qt=qt),
        grid=(b, t // qt),
        in_specs=[pl.BlockSpec((None, n_pairs, qt, LANES), lambda i, j: (i, 0, j, 0)),
                  pl.BlockSpec((None, n_pairs, t, LANES), lambda i, j: (i, 0, 0, 0)),
                  pl.BlockSpec((None, n_pairs, t, LANES), lambda i, j: (i, 0, 0, 0)),
                  pl.BlockSpec((2, BLK, 2 * BLK), lambda i, j: (0, 0, 0))],
        out_specs=pl.BlockSpec((None, qt, n_pairs * LANES), lambda i, j: (i, j, 0)),
        out_shape=jax.ShapeDtypeStruct((b, t, n_pairs * LANES), BF16),
        scratch_shapes=[pltpu.VMEM((qt, LANES), F32)] * (3 * n_pairs),
        compiler_params=_params(2),
        name="dil",
    )(q, k, v, jnp.asarray(_band_bias()))


def _shift_and_stage(cache_ref, new_ref, out_ref, stage, s_len):
    c, lb = cache_ref.shape
    new = new_ref[...]
    new_t = jnp.concatenate([new, jnp.zeros((LANES - s_len, c), F32)], axis=0).T
    cache = cache_ref[...]
    stage[:, 0:lb] = cache.astype(BF16)
    stage[:, lb:] = new_t.astype(BF16)
    shifted = pltpu.roll(cache, lb - s_len, 1)
    if lb > LANES:
        out_ref[:, 0:lb - LANES] = shifted[:, 0:lb - LANES]
    keep = lax.broadcasted_iota(jnp.int32, (c, LANES), 1) < LANES - s_len
    out_ref[:, lb - LANES:] = jnp.where(keep, shifted[:, lb - LANES:], pltpu.roll(new_t, LANES - s_len, 1))


def _scores_t(qblk, keys_t):
    return jnp.dot(qblk.astype(BF16), keys_t, preferred_element_type=F32)


def _weighted_t(p, values_t):
    return lax.dot_general(p.astype(BF16), values_t, (((1,), (1,)), ((), ())), preferred_element_type=F32)


def _sample_small_kernel(qa_ref, kan_ref, van_ref, qx_ref, cak_ref, cav_ref, cmk_ref, cmv_ref, sink_ref, amask_ref,
                         oa_ref, ox_ref, sak_ref, sav_ref, ka_s, va_s, *, s_len):
    lo = _lane_lo((s_len, LANES))
    for b in range(cak_ref.shape[0]):
        tok = slice(b * s_len, (b + 1) * s_len)
        _shift_and_stage(cak_ref.at[b], kan_ref.at[tok], sak_ref.at[b], ka_s.at[b], s_len)
        _shift_and_stage(cav_ref.at[b], van_ref.at[tok], sav_ref.at[b], va_s.at[b], s_len)

        qa = qa_ref[tok, :]
        pieces = []
        for p in range(Q_A // LANES):
            tile = qa[:, p * LANES:(p + 1) * LANES]
            pieces += [jnp.where(lo, tile, 0.0), jnp.where(lo, 0.0, tile)]
        s = _scores_t(jnp.concatenate(pieces, axis=0), ka_s[b])
        s = jnp.where(amask_ref[...] > 0.0, s, -jnp.inf)
        m = jnp.max(s, axis=1, keepdims=True)
        p_ = jnp.exp(s - m)
        l = jnp.sum(p_, axis=1, keepdims=True)
        o = _weighted_t(p_, va_s[b])
        sink = sink_ref[...]
        mm = jnp.maximum(m, sink)
        a = jnp.exp(m - mm)
        o = o * (a / (l * a + jnp.exp(sink - mm)))
        oa_ref[tok, :] = jnp.concatenate(
            [jnp.where(lo, o[2 * p * s_len:(2 * p + 1) * s_len], o[(2 * p + 1) * s_len:(2 * p + 2) * s_len])
             for p in range(Q_A // LANES)], axis=1)

        qx = jnp.concatenate([qx_ref[tok, :]] * MEM_HEADS, axis=0)
        row_h = lax.broadcasted_iota(jnp.int32, qx.shape, 0) // s_len
        col_h = lax.broadcasted_iota(jnp.int32, qx.shape, 1) // HEAD_DIM
        own = row_h == col_h
        s = _scores_t(jnp.where(own, qx, 0.0), cmk_ref[b].astype(BF16))
        m = jnp.max(s, axis=1, keepdims=True)
        p_ = jnp.exp(s - m)
        l = jnp.sum(p_, axis=1, keepdims=True)
        o_all = jnp.where(own, _weighted_t(p_, cmv_ref[b].astype(BF16)) / l, 0.0)
        o = o_all[0:s_len]
        for h in range(1, MEM_HEADS):
            o = o + o_all[h * s_len:(h + 1) * s_len]
        ox_ref[tok, :] = o


def _sample_tables(s_len, lb_a, lb_b, na, nb):
    i = np.arange(s_len)[:, None]
    dist_a = lb_a + i - np.arange(na)[None, :]
    amask = ((dist_a >= 0) & (dist_a <= SWA_WINDOW)).astype(np.float32)
    dist_b = lb_b + i - np.arange(nb)[None, :]
    cnt = np.zeros(dist_b.shape, np.float32)
    for w, r in DIL_PATTERNS:
        cnt += ((dist_b >= 0) & (dist_b <= w) & (dist_b % r == 0)).astype(np.float32)
    return np.tile(amask, (SWA_KV_HEADS * SWA_GROUP, 1)), np.tile(cnt, (2, 1))


def _sample_small(qa, kan, van, qx, cak, cav, cmk, cmv, sink_rows, amask, *, s_len, group):
    nbat, _, lb_a = cak.shape
    mtok = cmk.shape[2]
    na = amask.shape[1]
    assert nbat % group == 0
    tok = lambda c: pl.BlockSpec((group * s_len, c), lambda n: (n, 0))
    per_b = lambda r, c: pl.BlockSpec((group, r, c), lambda n: (n, 0, 0))
    const = lambda a: pl.BlockSpec(a.shape, lambda n: (0, 0))
    n_tok = nbat * s_len
    return pl.pallas_call(
        functools.partial(_sample_small_kernel, s_len=s_len),
        grid=(nbat // group,),
        in_specs=[tok(Q_A), tok(KV_A), tok(KV_A), tok(Q_X),
                  per_b(KV_A, lb_a), per_b(KV_A, lb_a), per_b(Q_X, mtok), per_b(Q_X, mtok),
                  const(sink_rows), const(amask)],
        out_specs=[tok(Q_A), tok(Q_X), per_b(KV_A, lb_a), per_b(KV_A, lb_a)],
        out_shape=[jax.ShapeDtypeStruct((n_tok, Q_A), F32), jax.ShapeDtypeStruct((n_tok, Q_X), F32),
                   jax.ShapeDtypeStruct(cak.shape, F32), jax.ShapeDtypeStruct(cav.shape, F32)],
        scratch_shapes=[pltpu.VMEM((group, KV_A, na), BF16), pltpu.VMEM((group, KV_A, na), BF16)],
        compiler_params=_params(1),
        name="sample_small",
    )(qa, kan, van, qx, cak, cav, cmk, cmv, sink_rows, amask)


def _new_positions_t(new_ref, s_len):
    c = new_ref.shape[1]
    return jnp.concatenate([new_ref[...], jnp.zeros((LANES - s_len, c), F32)], axis=0).T


def _shifted(cache, new_t, s_len):
    c, lb = cache.shape
    rolled = pltpu.roll(cache, lb - s_len, 1)
    keep = lax.broadcasted_iota(jnp.int32, (c, LANES), 1) < LANES - s_len
    tail = jnp.where(keep, rolled[:, lb - LANES:], pltpu.roll(new_t, LANES - s_len, 1))
    return jnp.concatenate([rolled[:, 0:lb - LANES], tail], axis=1) if lb > LANES else tail


def _dil_unit(qn_ref, kn_ref, vn_ref, ck_ref, cv_ref, cnt_ref, obs_ref, sk_ref, sv_ref, s_len):
    lb = ck_ref.shape[1]
    kc, vc = ck_ref[...], cv_ref[...]
    kn_t, vn_t = _new_positions_t(kn_ref, s_len), _new_positions_t(vn_ref, s_len)
    qn = qn_ref[...]
    lo = _lane_lo(qn.shape)
    qblk = jnp.concatenate([jnp.where(lo, qn, 0.0), jnp.where(lo, 0.0, qn)], axis=0)
    cnt_c, cnt_n = cnt_ref[:, 0:lb], cnt_ref[:, lb:]
    s_c = jnp.where(cnt_c > 0.0, _scores_t(qblk, kc.astype(BF16)), -jnp.inf)
    s_n = jnp.where(cnt_n > 0.0, _scores_t(qblk, kn_t.astype(BF16)), -jnp.inf)
    m = jnp.maximum(jnp.max(s_c, axis=1, keepdims=True), jnp.max(s_n, axis=1, keepdims=True))
    p_c = cnt_c * jnp.exp(s_c - m)
    p_n = cnt_n * jnp.exp(s_n - m)
    l = jnp.sum(p_c, axis=1, keepdims=True) + jnp.sum(p_n, axis=1, keepdims=True)
    o = (_weighted_t(p_c, vc.astype(BF16)) + _weighted_t(p_n, vn_t.astype(BF16))) / l
    obs_ref[...] = jnp.where(lo, o[0:s_len], o[s_len:2 * s_len])
    sk_ref[...] = _shifted(kc, kn_t, s_len)
    sv_ref[...] = _shifted(vc, vn_t, s_len)


def _mix_out(x_ref, oa_ref, ob_ref, ox_ref, wo_ref, g1_ref, g2_ref):
    cat = jnp.concatenate([oa_ref[...].astype(BF16), ob_ref[...].astype(BF16), ox_ref[...].astype(BF16)], axis=1)
    x = x_ref[...] + _rms(jnp.dot(cat, wo_ref[...], preferred_element_type=F32), g1_ref[...])
    return x, _rms(x, g2_ref[...]).astype(BF16)


def _gate_up(h, wg, wu):
    g = jnp.dot(h, wg, preferred_element_type=F32)
    u = jnp.dot(h, wu, preferred_element_type=F32)
    return (g * (1.0 / (1.0 + jnp.exp(-g))) * u).astype(BF16)


def _ffn_chunk(h, wg, wu, wd):
    return jnp.dot(_gate_up(h, wg, wu), wd, preferred_element_type=F32)


def _finish_kernel(x_ref, oa_ref, ob_ref, ox_ref, wo_ref, g1_ref, g2_ref, wg_ref, wu_ref, wd_ref, g3_ref, out_ref,
                   *, ff_chunk):
    x, h = _mix_out(x_ref, oa_ref, ob_ref, ox_ref, wo_ref, g1_ref, g2_ref)
    f = 0.0
    for c0 in range(0, wg_ref.shape[1], ff_chunk):
        cols = slice(c0, c0 + ff_chunk)
        f = f + _ffn_chunk(h, wg_ref[:, cols], wu_ref[:, cols], wd_ref[cols, :])
    out_ref[...] = x + _rms(f, g3_ref[...])


def _finish(x, oa, ob, ox, wo, g1, g2, wg, wu, wd, g3, *, tm, ff_chunk):
    n, d = x.shape
    row = lambda c: pl.BlockSpec((tm, c), lambda i: (i, 0))
    const = lambda a: pl.BlockSpec(a.shape, lambda i: (0,) * a.ndim, pipeline_mode=pl.Buffered(1))
    return pl.pallas_call(
        functools.partial(_finish_kernel, ff_chunk=ff_chunk),
        grid=(n // tm,),
        in_specs=[row(d), row(Q_A), row(Q_B), row(Q_X)] + [const(a) for a in (wo, g1, g2, wg, wu, wd, g3)],
        out_specs=row(d),
        out_shape=jax.ShapeDtypeStruct((n, d), F32),
        compiler_params=_params(1),
        name="finish",
    )(x, oa, ob, ox, wo, g1, g2, wg, wu, wd, g3)


def _finish_shift_kernel(x_ref, oa_ref, ob_ref, ox_ref, wo_ref, g1_ref, g2_ref, wg_ref, wu_ref, wd_ref, g3_ref,
                         qn_ref, kn_ref, vn_ref, ck_hbm, cv_hbm, cnt_ref,
                         out_ref, obs_ref, sk_hbm, sv_hbm,
                         x1_s, h_s, f_s, act_s, kin, vin, kout, vout, sem_in, sem_out, *, s_len, ff_chunk):
    i = pl.program_id(0)
    n_tiles = pl.num_programs(0)
    r_in, r_out = kin.shape[0], kout.shape[0]
    d_ff = wg_ref.shape[1]
    n_sub = d_ff // ff_chunk + 1
    assert n_sub % r_in == 0 and n_sub % r_out == 0, "ring slots must line up across token tiles"
    base = i * n_sub

    def fetch(u, s):
        return (pltpu.make_async_copy(ck_hbm.at[u], kin.at[s], sem_in.at[0, s]),
                pltpu.make_async_copy(cv_hbm.at[u], vin.at[s], sem_in.at[1, s]))

    def flush(u, s):
        return (pltpu.make_async_copy(kout.at[s], sk_hbm.at[u], sem_out.at[0, s]),
                pltpu.make_async_copy(vout.at[s], sv_hbm.at[u], sem_out.at[1, s]))

    @pl.when(i == 0)
    def _():
        for k in range(r_in):
            for cp in fetch(k, k):
                cp.start()

    @pl.when(i > 0)
    def _():
        for cc in range(n_sub - r_out, n_sub):
            for cp in flush(base - n_sub + cc, cc % r_out):
                cp.wait()

    for cc in range(n_sub):
        u = base + cc
        s_in, s_out = cc % r_in, cc % r_out
        for cp in fetch(u, s_in):
            cp.wait()
        if cc >= r_out:
            for cp in flush(u - r_out, s_out):
                cp.wait()
        _dil_unit(qn_ref.at[u], kn_ref.at[u], vn_ref.at[u], kin.at[s_in], vin.at[s_in], cnt_ref,
                  obs_ref.at[u], kout.at[s_out], vout.at[s_out], s_len)
        prev = slice((cc - 1) * ff_chunk, cc * ff_chunk)
        cur = slice(cc * ff_chunk, (cc + 1) * ff_chunk)
        if cc == 0:
            x, h = _mix_out(x_ref, oa_ref, ob_ref, ox_ref, wo_ref, g1_ref, g2_ref)
            x1_s[...] = x
            h_s[...] = h
            act_s[...] = _gate_up(h, wg_ref[:, cur], wu_ref[:, cur])
        elif cc < n_sub - 1:
            down = jnp.dot(act_s[...], wd_ref[prev, :], preferred_element_type=F32)
            act_s[...] = _gate_up(h_s[...], wg_ref[:, cur], wu_ref[:, cur])
            f_s[...] = down if cc == 1 else f_s[...] + down
        else:
            f = f_s[...] + jnp.dot(act_s[...], wd_ref[prev, :], preferred_element_type=F32)
            out_ref[...] = x1_s[...] + _rms(f, g3_ref[...])
        for cp in flush(u, s_out):
            cp.start()
        if cc + r_in < n_sub:
            for cp in fetch(u + r_in, s_in):
                cp.start()

    @pl.when(i + 1 < n_tiles)
    def _():
        for k in range(r_in):
            for cp in fetch(base + n_sub + k, k):
                cp.start()

    @pl.when(i + 1 == n_tiles)
    def _():
        for cc in range(n_sub - r_out, n_sub):
            for cp in flush(base + cc, cc % r_out):
                cp.wait()


IN_RING = 3
OUT_RING = 2


def _finish_shift(x, oa, ob, ox, wo, g1, g2, wg, wu, wd, g3, qn, kn, vn, ck, cv, cnt, *, tm, s_len, ff_chunk):
    n, d = x.shape
    n_sub = wg.shape[1] // ff_chunk + 1
    n_units, cw, lb = ck.shape
    assert n_units == (n // tm) * n_sub, "one cache unit per sub-step"
    row = lambda c: pl.BlockSpec((tm, c), lambda i: (i, 0))
    const = lambda a: pl.BlockSpec(a.shape, lambda i: (0,) * a.ndim, pipeline_mode=pl.Buffered(1))
    whole_out = pl.BlockSpec(qn.shape, lambda i: (0, 0, 0))
    hbm = pl.BlockSpec(memory_space=pl.ANY)
    ring = lambda r: pltpu.VMEM((r, cw, lb), F32)
    return pl.pallas_call(
        functools.partial(_finish_shift_kernel, s_len=s_len, ff_chunk=ff_chunk),
        grid=(n // tm,),
        in_specs=[row(d), row(Q_A), row(Q_B), row(Q_X)] + [const(a) for a in (wo, g1, g2, wg, wu, wd, g3)]
                 + [const(a) for a in (qn, kn, vn)] + [hbm] * 2 + [const(cnt)],
        out_specs=[row(d), whole_out, hbm, hbm],
        out_shape=[jax.ShapeDtypeStruct((n, d), F32), jax.ShapeDtypeStruct(qn.shape, F32),
                   jax.ShapeDtypeStruct(ck.shape, F32), jax.ShapeDtypeStruct(cv.shape, F32)],
        scratch_shapes=[pltpu.VMEM((tm, d), F32), pltpu.VMEM((tm, d), BF16), pltpu.VMEM((tm, d), F32),
                        pltpu.VMEM((tm, ff_chunk), BF16),
                        ring(IN_RING), ring(IN_RING), ring(OUT_RING), ring(OUT_RING),
                        pltpu.SemaphoreType.DMA((2, IN_RING)), pltpu.SemaphoreType.DMA((2, OUT_RING))],
        compiler_params=_params(1),
        name="finish_shift",
    )(x, oa, ob, ox, wo, g1, g2, wg, wu, wd, g3, qn, kn, vn, ck, cv, cnt)


def _rope_tables(pos):
    half = HEAD_DIM // 2
    inv = ROPE_THETA ** (-jnp.arange(half, dtype=F32) / half)
    ang = pos.astype(F32)[:, None] * inv[None, :]
    cos = jnp.tile(jnp.cos(ang), (1, LANES // half))
    sin = jnp.tile(jnp.sin(ang), (1, LANES // half))
    first_half = (jnp.arange(LANES) % HEAD_DIM) < half
    return cos, jnp.where(first_half[None, :], -sin, sin)


def _swa_head_perm():
    return np.concatenate([np.arange(h * HEAD_DIM, (h + 1) * HEAD_DIM) for h in SWA_HEAD_ORDER])


def _seq_minor(c):
    b, n, h, dh = c.shape
    return jnp.transpose(c, (0, 2, 3, 1)).reshape(b, h * dh, n)


def _seq_major(a, heads):
    b, c, n = a.shape
    return jnp.transpose(a.reshape(b, heads, c // heads, n), (0, 3, 1, 2))


def kernel(x_prompt, x_sample, cache_swa_k, cache_swa_v, cache_dil_k, cache_dil_v, cache_mem_k, cache_mem_v,
           mem_prompt, g_pre_mix, w_in, sinks, w_mem_kv, w_o, g_post_mix, g_pre_ffn, w_gate, w_up, w_down,
           g_post_ffn):
    depth = w_in.shape[0]
    bp, t_p, d = x_prompt.shape
    bs, s_len, _ = x_sample.shape
    d_ff = w_gate.shape[2]
    perm = _swa_head_perm()
    cos_p, sin_p = _rope_tables(jnp.arange(t_p))
    cos_s, sin_s = _rope_tables(PAST_LEN + jnp.arange(s_len))
    tm_s = 256
    cos_s = jnp.tile(cos_s, (tm_s // s_len, 1))
    sin_s = jnp.tile(sin_s, (tm_s // s_len, 1))
    ff_chunk = 2 * LANES
    assert d_ff % ff_chunk == 0
    n_pairs = Q_B // LANES
    lb_a = cache_swa_k.shape[2]
    lb_b = cache_dil_k.shape[2]
    assert lb_a >= SWA_WINDOW and lb_b >= DIL_MAX_WINDOW, "every window position must lie inside the cache"
    assert lb_a % LANES == 0 and lb_b % LANES == 0 and s_len <= LANES
    amask, bcnt = (jnp.asarray(a) for a in _sample_tables(s_len, lb_a, lb_b, lb_a + LANES, lb_b + LANES))
    n_a = min(SWA_WINDOW, t_p)
    n_b = min(DIL_MAX_WINDOW, t_p)

    hp = x_prompt
    hs = x_sample.reshape(1, bs * s_len, d)
    outs = [[] for _ in range(10)]
    for l in range(depth):
        w_in_l = w_in[l]
        w_in_bf = jnp.concatenate([w_in_l[:, :Q_A][:, perm], w_in_l[:, Q_A:]], axis=1).astype(BF16)
        w_o_bf = jnp.concatenate([w_o[l][:Q_A][perm], w_o[l][Q_A:]], axis=0).astype(BF16)
        w_mem_bf = w_mem_kv[l].astype(BF16)
        wg, wu, wd = w_gate[l].astype(BF16), w_up[l].astype(BF16), w_down[l].astype(BF16)
        g0, g1, g2, g3 = (g[l][None, :] for g in (g_pre_mix, g_post_mix, g_pre_ffn, g_post_ffn))
        sink_heads = sinks[l].reshape(-1)[np.array(SWA_HEAD_ORDER)].astype(F32)
        sink_row = jnp.repeat(sink_heads, HEAD_DIM)[None, :]
        sink_rows = jnp.repeat(sink_heads, s_len)[:, None]

        qa, qb, qx, kah, vah, kbf, vbf, kat, vat, kbt, vbt = _proj(
            hp, g0, w_in_bf, cos_p, sin_p, tm=512, prompt=True, n_a=n_a, n_b=n_b)
        qa_s, qb_s, qx_s, kan, van, kbn, vbn = (a[0] for a in _proj(
            hs, g0, w_in_bf, cos_s, sin_s, tm=tm_s, prompt=False))

        mkt, mvt, mkh, mvh = _memkv(mem_prompt, w_mem_bf)
        oa, ox = _swa_mem(qa, kah, vah, qx, mkh, mvh, sink_row, qt=1024)
        ob = _dil(qb, kbf, vbf, qt=2048)

        to_units = lambda a: jnp.transpose(a.reshape(bs, s_len, n_pairs, LANES), (0, 2, 1, 3)).reshape(
            bs * n_pairs, s_len, LANES)
        cache_units = lambda c: _seq_minor(c).reshape(bs * n_pairs, LANES, lb_b)
        flat = lambda a: a.reshape(bp * t_p, a.shape[-1])
        hp, ob_s, sbk, sbv = _finish_shift(
            flat(hp), flat(oa), flat(ob), flat(ox), w_o_bf, g1, g2, wg, wu, wd, g3,
            to_units(qb_s), to_units(kbn), to_units(vbn), cache_units(cache_dil_k[l]), cache_units(cache_dil_v[l]),
            bcnt, tm=512, s_len=s_len, ff_chunk=ff_chunk)
        hp = hp.reshape(bp, t_p, d)
        ob_s = jnp.transpose(ob_s.reshape(bs, n_pairs, s_len, LANES), (0, 2, 1, 3)).reshape(bs * s_len, Q_B)
        outs[0].append(_seq_major(kat, SWA_KV_HEADS))
        outs[1].append(_seq_major(vat, SWA_KV_HEADS))
        outs[2].append(_seq_major(kbt, DIL_HEADS))
        outs[3].append(_seq_major(vbt, DIL_HEADS))
        outs[4].append(_seq_major(mkt, MEM_HEADS))
        outs[5].append(_seq_major(mvt, MEM_HEADS))

        oa_s, ox_s, sak, sav = _sample_small(
            qa_s, kan, van, qx_s, _seq_minor(cache_swa_k[l]), _seq_minor(cache_swa_v[l]),
            _seq_minor(cache_mem_k[l]), _seq_minor(cache_mem_v[l]), sink_rows, amask, s_len=s_len, group=8)
        hs = _finish(hs[0], oa_s, ob_s, ox_s, w_o_bf, g1, g2, wg, wu, wd, g3, tm=256, ff_chunk=ff_chunk)[None]
        outs[6].append(_seq_major(sak, SWA_KV_HEADS))
        outs[7].append(_seq_major(sav, SWA_KV_HEADS))
        outs[8].append(_seq_major(sbk.reshape(bs, KV_B, lb_b), DIL_HEADS))
        outs[9].append(_seq_major(sbv.reshape(bs, KV_B, lb_b), DIL_HEADS))

    return (hp, hs.reshape(bs, s_len, d)) + tuple(jnp.stack(o) for o in outs)
```

```python
import functools

import numpy as np
import jax
import jax.numpy as jnp
from jax import lax
from jax.experimental import pallas as pl
from jax.experimental.pallas import tpu as pltpu

F32 = jnp.float32
BF16 = jnp.bfloat16

HEAD_DIM = 64
SWA_KV_HEADS = 2
SWA_GROUP = 3
SWA_WINDOW = 128
DIL_HEADS = 6
DIL_PATTERNS = ((128, 1), (512, 4), (2048, 16))
DIL_MAX_WINDOW = 2048
MEM_HEADS = 4
BLK = 128
ROPE_THETA = 10000.0
EPS = 1e-6
SCALE = HEAD_DIM ** -0.5
PAST_LEN = 16384

Q_A = SWA_KV_HEADS * SWA_GROUP * HEAD_DIM
KV_A = SWA_KV_HEADS * HEAD_DIM
Q_B = DIL_HEADS * HEAD_DIM
KV_B = DIL_HEADS * HEAD_DIM
Q_X = MEM_HEADS * HEAD_DIM
IN_SIZES = (Q_A, KV_A, KV_A, Q_B, KV_B, KV_B, Q_X)
IN_OFFS = tuple(int(v) for v in np.cumsum((0,) + IN_SIZES))

LANES = 128
VMEM_LIMIT = 56 * 1024 * 1024

SWA_HEAD_ORDER = (0, 3, 1, 4, 2, 5)


def _params(n_axes):
    return pltpu.CompilerParams(dimension_semantics=("arbitrary",) * n_axes,
                                vmem_limit_bytes=VMEM_LIMIT)


def _rms(x, g):
    return x * lax.rsqrt(jnp.mean(x * x, axis=-1, keepdims=True) + EPS) * g


def _lane_lo(shape):
    return lax.broadcasted_iota(jnp.int32, shape, len(shape) - 1) % LANES < HEAD_DIM


def _proj_kernel(x_ref, g_ref, w_ref, cos_ref, sin_ref, *out_refs, prompt, n_a, n_b):
    tm = x_ref.shape[0]
    u = _rms(x_ref[...], g_ref[...]).astype(BF16)
    cos = cos_ref[...]
    sin = sin_ref[...]
    first_half = lax.broadcasted_iota(jnp.int32, cos.shape, 1) % HEAD_DIM < HEAD_DIM // 2

    z = jnp.dot(u, w_ref[...], preferred_element_type=F32)

    def proj(i):
        return z[:, IN_OFFS[i]:IN_OFFS[i + 1]]

    def rope(z):
        tiles = []
        for s in range(z.shape[1] // LANES):
            zs = z[:, s * LANES:(s + 1) * LANES]
            partner = jnp.where(first_half, pltpu.roll(zs, LANES - HEAD_DIM // 2, 1),
                                pltpu.roll(zs, HEAD_DIM // 2, 1))
            tiles.append(zs * cos + partner * sin)
        return jnp.concatenate(tiles, axis=1) if len(tiles) > 1 else tiles[0]

    qa = rope(proj(0))
    ka = rope(proj(1))
    va = proj(2)
    qb = rope(proj(3))
    kb = rope(proj(4))
    vb = proj(5)
    qx = proj(6)
    qa_ref, qb_ref, qx_ref, ka_ref, va_ref, kb_ref, vb_ref = out_refs[:7]
    qa_ref[...] = (qa * SCALE).astype(qa_ref.dtype)
    qx_ref[...] = (qx * SCALE).astype(qx_ref.dtype)
    ka_ref[...] = ka.astype(ka_ref.dtype)
    va_ref[...] = va.astype(va_ref.dtype)
    if not prompt:
        qb_ref[...] = qb * SCALE
        kb_ref[...] = kb
        vb_ref[...] = vb
    else:
        for p in range(Q_B // LANES):
            cols = slice(p * LANES, (p + 1) * LANES)
            qb_ref[p] = qb[:, cols] * SCALE
            kb_ref[p] = kb[:, cols]
            vb_ref[p] = vb[:, cols]
        kat_ref, vat_ref, kbt_ref, vbt_ref = out_refs[7:]
        j = pl.program_id(1)
        nj = pl.num_programs(1)

        @pl.when(j >= nj - n_b // tm)
        def _():
            kbt_ref[...] = kb.T
            vbt_ref[...] = vb.T

        @pl.when(j == nj - 1)
        def _():
            kat_ref[...] = ka[tm - n_a:, :].T
            vat_ref[...] = va[tm - n_a:, :].T


def _proj(x, g, w_bf, cos, sin, *, tm, prompt, n_a=0, n_b=0):
    b, t, d = x.shape
    nj = t // tm
    n_rope = cos.shape[0] // tm
    tile = lambda c: pl.BlockSpec((None, tm, c), lambda i, j: (i, j, 0))
    rope = pl.BlockSpec((tm, LANES), lambda i, j: (j % n_rope, 0))
    dt = BF16 if prompt else F32
    widths = (Q_A, Q_B, Q_X, KV_A, KV_A, KV_B, KV_B)
    out_specs = [tile(c) for c in widths]
    out_shape = [jax.ShapeDtypeStruct((b, t, c), dt) for c in widths]
    if prompt:
        assert n_a <= tm and n_a % LANES == 0 and n_b % tm == 0 and n_b <= t
        n_pairs = Q_B // LANES
        for i in (1, 5, 6):
            out_specs[i] = pl.BlockSpec((None, n_pairs, tm, LANES), lambda i, j: (i, 0, j, 0))
            out_shape[i] = jax.ShapeDtypeStruct((b, n_pairs, t, LANES), F32)
        first_b = nj - n_b // tm
        out_specs += [pl.BlockSpec((None, KV_A, n_a), lambda i, j: (i, 0, 0))] * 2
        out_specs += [pl.BlockSpec((None, KV_B, tm), lambda i, j: (i, 0, jnp.maximum(j - first_b, 0)))] * 2
        out_shape += [jax.ShapeDtypeStruct((b, KV_A, n_a), F32)] * 2 + [jax.ShapeDtypeStruct((b, KV_B, n_b), F32)] * 2
    return pl.pallas_call(
        functools.partial(_proj_kernel, prompt=prompt, n_a=n_a, n_b=n_b),
        grid=(b, nj),
        in_specs=[tile(d), pl.BlockSpec((1, d), lambda i, j: (0, 0)),
                  pl.BlockSpec(w_bf.shape, lambda i, j: (0, 0), pipeline_mode=pl.Buffered(1)),
                  rope, rope],
        out_specs=out_specs,
        out_shape=out_shape,
        compiler_params=_params(2),
        name="proj_prompt" if prompt else "proj_sample",
    )(x, g, w_bf, cos, sin)


def _memkv_kernel(mem_ref, w_ref, mkt_ref, mvt_ref, mkh_ref, mvh_ref):
    z = jnp.dot(mem_ref[...].astype(BF16), w_ref[...], preferred_element_type=F32)
    mk = z[:, :Q_X]
    mv = z[:, Q_X:]
    mkt_ref[...] = mk.T
    mvt_ref[...] = mv.T
    mkh_ref[...] = mk.astype(BF16)
    mvh_ref[...] = mv.astype(BF16)


def _memkv(mem, w_bf):
    b, m, d = mem.shape
    blk = lambda r, c: pl.BlockSpec((None, r, c), lambda i: (i, 0, 0))
    return pl.pallas_call(
        _memkv_kernel,
        grid=(b,),
        in_specs=[blk(m, d), pl.BlockSpec(w_bf.shape, lambda i: (0, 0))],
        out_specs=[blk(Q_X, m)] * 2 + [blk(m, Q_X)] * 2,
        out_shape=[jax.ShapeDtypeStruct((b, Q_X, m), F32)] * 2 + [jax.ShapeDtypeStruct((b, m, Q_X), BF16)] * 2,
        compiler_params=_params(1),
        name="memkv",
    )(mem, w_bf)


def _pair_attend(q, kk, vv, bias):
    tq = q.shape[0]
    zero = jnp.zeros_like(q)
    q_lo = _lane_lo(q.shape)
    q2 = jnp.concatenate([jnp.where(q_lo, q, zero), jnp.where(q_lo, zero, q)], axis=0)
    s = lax.dot_general(q2, kk, (((1,), (1,)), ((), ())), preferred_element_type=F32)
    stats = []
    for h in range(2):
        sh = s[h * tq:(h + 1) * tq]
        if bias is not None:
            sh = sh + bias
        m = jnp.max(sh, axis=1, keepdims=True)
        p = jnp.exp(sh - m)
        stats.append((m, jnp.sum(p, axis=1, keepdims=True), p.astype(BF16)))
    v_lo = _lane_lo(vv.shape)
    vzero = jnp.zeros_like(vv)
    o = (jnp.dot(stats[0][2], jnp.where(v_lo, vv, vzero), preferred_element_type=F32)
         + jnp.dot(stats[1][2], jnp.where(v_lo, vzero, vv), preferred_element_type=F32))
    o_lo = _lane_lo(o.shape)
    m = jnp.where(o_lo, stats[0][0], stats[1][0])
    l = jnp.where(o_lo, stats[0][1], stats[1][1])
    return o, m, l


def _band_bias():
    qi = np.arange(BLK)[:, None]
    c = np.arange(2 * BLK)[None, :]
    in_prev = (c < BLK) & (c >= qi)
    in_cur = (c >= BLK) & (c - BLK <= qi)
    allowed = np.stack([in_cur, in_prev | in_cur])
    return np.where(allowed, 0.0, -np.inf).astype(np.float32)


def _band_rows(j, i, n_blk):
    n = j * n_blk + i
    r0 = pl.multiple_of(i * BLK, BLK)
    k_cur = pl.multiple_of(n * BLK, BLK)
    k_prev = pl.multiple_of(jnp.maximum(n - 1, 0) * BLK, BLK)
    return n, r0, k_prev, k_cur


def _swa_mem_kernel(qa_ref, ka_ref, va_ref, qx_ref, mk_ref, mv_ref, sink_ref, bias_ref, oa_ref, ox_ref, *, qt):
    j = pl.program_id(1)
    n_blk = qt // BLK

    def body(i, carry):
        n, r0, k_prev, k_cur = _band_rows(j, i, n_blk)
        rows = pl.ds(r0, BLK)
        mask = bias_ref[jnp.minimum(n, 1)]
        kk = jnp.concatenate([ka_ref[pl.ds(k_prev, BLK), :], ka_ref[pl.ds(k_cur, BLK), :]], axis=0)
        vv = jnp.concatenate([va_ref[pl.ds(k_prev, BLK), :], va_ref[pl.ds(k_cur, BLK), :]], axis=0)
        for p in range(Q_A // LANES):
            cols = slice(p * LANES, (p + 1) * LANES)
            o, m, l = _pair_attend(qa_ref[rows, cols], kk, vv, mask)
            sink = sink_ref[:, cols]
            mm = jnp.maximum(m, sink)
            a = jnp.exp(m - mm)
            den = l * a + jnp.exp(sink - mm)
            oa_ref[rows, cols] = (o * (a / den)).astype(oa_ref.dtype)
        for p in range(Q_X // LANES):
            cols = slice(p * LANES, (p + 1) * LANES)
            o, m, l = _pair_attend(qx_ref[rows, cols], mk_ref[:, cols], mv_ref[:, cols], None)
            ox_ref[rows, cols] = (o / l).astype(ox_ref.dtype)
        return carry

    lax.fori_loop(0, n_blk, body, 0, unroll=4)


def _swa_mem(qa, ka, va, qx, mk, mv, sink_row, *, qt):
    b, t, _ = qa.shape
    m = mk.shape[1]
    qtile = lambda c: pl.BlockSpec((None, qt, c), lambda i, j: (i, j, 0))
    whole = lambda r, c: pl.BlockSpec((None, r, c), lambda i, j: (i, 0, 0))
    return pl.pallas_call(
        functools.partial(_swa_mem_kernel, qt=qt),
        grid=(b, t // qt),
        in_specs=[qtile(Q_A), whole(t, KV_A), whole(t, KV_A), qtile(Q_X), whole(m, Q_X), whole(m, Q_X),
                  pl.BlockSpec((1, Q_A), lambda i, j: (0, 0)),
                  pl.BlockSpec((2, BLK, 2 * BLK), lambda i, j: (0, 0, 0))],
        out_specs=[qtile(Q_A), qtile(Q_X)],
        out_shape=[jax.ShapeDtypeStruct((b, t, Q_A), BF16), jax.ShapeDtypeStruct((b, t, Q_X), BF16)],
        compiler_params=_params(2),
        name="swa_mem",
    )(qa, ka, va, qx, mk, mv, sink_row, jnp.asarray(_band_bias()))


def _dil_kernel(q_ref, k_ref, v_ref, bias_ref, ob_ref, *acc, qt):
    j = pl.program_id(1)
    n_pairs = Q_B // LANES
    patterns = sorted(DIL_PATTERNS, key=lambda wr: -wr[1])
    for idx, (w, r) in enumerate(patterns):
        assert w // r == BLK and qt % (r * BLK) == 0
        n_blk = qt // (r * BLK)

        def body(step, carry, r=r, n_blk=n_blk, first=(idx == 0)):
            a = lax.div(step, n_blk)
            i = lax.rem(step, n_blk)
            n = j * n_blk + i

            def rows(blk):
                if r == 1:
                    return pl.ds(pl.multiple_of(blk * BLK, BLK), BLK)
                return pl.ds(blk * (BLK * r) + a, BLK, stride=r)

            prev = jnp.maximum(n - 1, 0)
            bias = bias_ref[jnp.minimum(n, 1)]
            for p in range(n_pairs):
                acc_o, acc_m, acc_l = acc[3 * p:3 * p + 3]
                q = q_ref[p, rows(i), :].astype(BF16)
                kk = jnp.concatenate([k_ref[p, rows(prev), :], k_ref[p, rows(n), :]], axis=0).astype(BF16)
                vv = jnp.concatenate([v_ref[p, rows(prev), :], v_ref[p, rows(n), :]], axis=0).astype(BF16)
                o, m, l = _pair_attend(q, kk, vv, bias)
                if not first:
                    m_old = acc_m[rows(i), :]
                    m_new = jnp.maximum(m_old, m)
                    e_old = jnp.exp(m_old - m_new)
                    e = jnp.exp(m - m_new)
                    o = acc_o[rows(i), :] * e_old + o * e
                    l = acc_l[rows(i), :] * e_old + l * e
                    m = m_new
                acc_o[rows(i), :] = o
                acc_m[rows(i), :] = m
                acc_l[rows(i), :] = l
            return carry

        lax.fori_loop(0, r * n_blk, body, 0, unroll=4)
    for p in range(n_pairs):
        ob_ref[:, p * LANES:(p + 1) * LANES] = (acc[3 * p][...] / acc[3 * p + 2][...]).astype(ob_ref.dtype)


def _dil(q, k, v, *, qt):
    b, n_pairs, t, _ = q.shape
    return pl.pallas_call(
        functools.partial(_dil_kernel, qt=qt),
        grid=(b, t // qt),
        in_specs=[pl.BlockSpec((None, n_pairs, qt, LANES), lambda i, j: (i, 0, j, 0)),
                  pl.BlockSpec((None, n_pairs, t, LANES), lambda i, j: (i, 0, 0, 0)),
                  pl.BlockSpec((None, n_pairs, t, LANES), lambda i, j: (i, 0, 0, 0)),
                  pl.BlockSpec((2, BLK, 2 * BLK), lambda i, j: (0, 0, 0))],
        out_specs=pl.BlockSpec((None, qt, n_pairs * LANES), lambda i, j: (i, j, 0)),
        out_shape=jax.ShapeDtypeStruct((b, t, n_pairs * LANES), BF16),
        scratch_shapes=[pltpu.VMEM((qt, LANES), F32)] * (3 * n_pairs),
        compiler_params=_params(2),
        name="dil",
    )(q, k, v, jnp.asarray(_band_bias()))


def _shift_and_stage(cache_ref, new_ref, out_ref, stage, s_len):
    c, lb = cache_ref.shape
    new = new_ref[...]
    new_t = jnp.concatenate([new, jnp.zeros((LANES - s_len, c), F32)], axis=0).T
    cache = cache_ref[...]
    stage[:, 0:lb] = cache.astype(BF16)
    stage[:, lb:] = new_t.astype(BF16)
    shifted = pltpu.roll(cache, lb - s_len, 1)
    if lb > LANES:
        out_ref[:, 0:lb - LANES] = shifted[:, 0:lb - LANES]
    keep = lax.broadcasted_iota(jnp.int32, (c, LANES), 1) < LANES - s_len
    out_ref[:, lb - LANES:] = jnp.where(keep, shifted[:, lb - LANES:], pltpu.roll(new_t, LANES - s_len, 1))


def _scores_t(qblk, keys_t):
    return jnp.dot(qblk.astype(BF16), keys_t, preferred_element_type=F32)


def _weighted_t(p, values_t):
    return lax.dot_general(p.astype(BF16), values_t, (((1,), (1,)), ((), ())), preferred_element_type=F32)


def _sample_small_kernel(qa_ref, kan_ref, van_ref, qx_ref, cak_ref, cav_ref, cmk_ref, cmv_ref, sink_ref, amask_ref,
                         oa_ref, ox_ref, sak_ref, sav_ref, ka_s, va_s, *, s_len):
    lo = _lane_lo((s_len, LANES))
    for b in range(cak_ref.shape[0]):
        tok = slice(b * s_len, (b + 1) * s_len)
        _shift_and_stage(cak_ref.at[b], kan_ref.at[tok], sak_ref.at[b], ka_s.at[b], s_len)
        _shift_and_stage(cav_ref.at[b], van_ref.at[tok], sav_ref.at[b], va_s.at[b], s_len)

        qa = qa_ref[tok, :]
        pieces = []
        for p in range(Q_A // LANES):
            tile = qa[:, p * LANES:(p + 1) * LANES]
            pieces += [jnp.where(lo, tile, 0.0), jnp.where(lo, 0.0, tile)]
        s = _scores_t(jnp.concatenate(pieces, axis=0), ka_s[b])
        s = jnp.where(amask_ref[...] > 0.0, s, -jnp.inf)
        m = jnp.max(s, axis=1, keepdims=True)
        p_ = jnp.exp(s - m)
        l = jnp.sum(p_, axis=1, keepdims=True)
        o = _weighted_t(p_, va_s[b])
        sink = sink_ref[...]
        mm = jnp.maximum(m, sink)
        a = jnp.exp(m - mm)
        o = o * (a / (l * a + jnp.exp(sink - mm)))
        oa_ref[tok, :] = jnp.concatenate(
            [jnp.where(lo, o[2 * p * s_len:(2 * p + 1) * s_len], o[(2 * p + 1) * s_len:(2 * p + 2) * s_len])
             for p in range(Q_A // LANES)], axis=1)

        qx = jnp.concatenate([qx_ref[tok, :]] * MEM_HEADS, axis=0)
        row_h = lax.broadcasted_iota(jnp.int32, qx.shape, 0) // s_len
        col_h = lax.broadcasted_iota(jnp.int32, qx.shape, 1) // HEAD_DIM
        own = row_h == col_h
        s = _scores_t(jnp.where(own, qx, 0.0), cmk_ref[b].astype(BF16))
        m = jnp.max(s, axis=1, keepdims=True)
        p_ = jnp.exp(s - m)
        l = jnp.sum(p_, axis=1, keepdims=True)
        o_all = jnp.where(own, _weighted_t(p_, cmv_ref[b].astype(BF16)) / l, 0.0)
        o = o_all[0:s_len]
        for h in range(1, MEM_HEADS):
            o = o + o_all[h * s_len:(h + 1) * s_len]
        ox_ref[tok, :] = o


def _sample_tables(s_len, lb_a, lb_b, na, nb):
    i = np.arange(s_len)[:, None]
    dist_a = lb_a + i - np.arange(na)[None, :]
    amask = ((dist_a >= 0) & (dist_a <= SWA_WINDOW)).astype(np.float32)
    dist_b = lb_b + i - np.arange(nb)[None, :]
    cnt = np.zeros(dist_b.shape, np.float32)
    for w, r in DIL_PATTERNS:
        cnt += ((dist_b >= 0) & (dist_b <= w) & (dist_b % r == 0)).astype(np.float32)
    return np.tile(amask, (SWA_KV_HEADS * SWA_GROUP, 1)), np.tile(cnt, (2, 1))


def _sample_small(qa, kan, van, qx, cak, cav, cmk, cmv, sink_rows, amask, *, s_len, group):
    nbat, _, lb_a = cak.shape
    mtok = cmk.shape[2]
    na = amask.shape[1]
    assert nbat % group == 0
    tok = lambda c: pl.BlockSpec((group * s_len, c), lambda n: (n, 0))
    per_b = lambda r, c: pl.BlockSpec((group, r, c), lambda n: (n, 0, 0))
    const = lambda a: pl.BlockSpec(a.shape, lambda n: (0, 0))
    n_tok = nbat * s_len
    return pl.pallas_call(
        functools.partial(_sample_small_kernel, s_len=s_len),
        grid=(nbat // group,),
        in_specs=[tok(Q_A), tok(KV_A), tok(KV_A), tok(Q_X),
                  per_b(KV_A, lb_a), per_b(KV_A, lb_a), per_b(Q_X, mtok), per_b(Q_X, mtok),
                  const(sink_rows), const(amask)],
        out_specs=[tok(Q_A), tok(Q_X), per_b(KV_A, lb_a), per_b(KV_A, lb_a)],
        out_shape=[jax.ShapeDtypeStruct((n_tok, Q_A), F32), jax.ShapeDtypeStruct((n_tok, Q_X), F32),
                   jax.ShapeDtypeStruct(cak.shape, F32), jax.ShapeDtypeStruct(cav.shape, F32)],
        scratch_shapes=[pltpu.VMEM((group, KV_A, na), BF16), pltpu.VMEM((group, KV_A, na), BF16)],
        compiler_params=_params(1),
        name="sample_small",
    )(qa, kan, van, qx, cak, cav, cmk, cmv, sink_rows, amask)


def _new_positions_t(new_ref, s_len):
    c = new_ref.shape[1]
    return jnp.concatenate([new_ref[...], jnp.zeros((LANES - s_len, c), F32)], axis=0).T


def _shifted(cache, new_t, s_len):
    c, lb = cache.shape
    rolled = pltpu.roll(cache, lb - s_len, 1)
    keep = lax.broadcasted_iota(jnp.int32, (c, LANES), 1) < LANES - s_len
    tail = jnp.where(keep, rolled[:, lb - LANES:], pltpu.roll(new_t, LANES - s_len, 1))
    return jnp.concatenate([rolled[:, 0:lb - LANES], tail], axis=1) if lb > LANES else tail


def _dil_unit(qn_ref, kn_ref, vn_ref, ck_ref, cv_ref, cnt_ref, obs_ref, sk_ref, sv_ref, s_len):
    lb = ck_ref.shape[1]
    kc, vc = ck_ref[...], cv_ref[...]
    kn_t, vn_t = _new_positions_t(kn_ref, s_len), _new_positions_t(vn_ref, s_len)
    qn = qn_ref[...]
    lo = _lane_lo(qn.shape)
    qblk = jnp.concatenate([jnp.where(lo, qn, 0.0), jnp.where(lo, 0.0, qn)], axis=0)
    cnt_c, cnt_n = cnt_ref[:, 0:lb], cnt_ref[:, lb:]
    s_c = jnp.where(cnt_c > 0.0, _scores_t(qblk, kc.astype(BF16)), -jnp.inf)
    s_n = jnp.where(cnt_n > 0.0, _scores_t(qblk, kn_t.astype(BF16)), -jnp.inf)
    m = jnp.maximum(jnp.max(s_c, axis=1, keepdims=True), jnp.max(s_n, axis=1, keepdims=True))
    p_c = cnt_c * jnp.exp(s_c - m)
    p_n = cnt_n * jnp.exp(s_n - m)
    l = jnp.sum(p_c, axis=1, keepdims=True) + jnp.sum(p_n, axis=1, keepdims=True)
    o = (_weighted_t(p_c, vc.astype(BF16)) + _weighted_t(p_n, vn_t.astype(BF16))) / l
    obs_ref[...] = jnp.where(lo, o[0:s_len], o[s_len:2 * s_len])
    sk_ref[...] = _shifted(kc, kn_t, s_len)
    sv_ref[...] = _shifted(vc, vn_t, s_len)


def _mix_out(x_ref, oa_ref, ob_ref, ox_ref, wo_ref, g1_ref, g2_ref):
    cat = jnp.concatenate([oa_ref[...].astype(BF16), ob_ref[...].astype(BF16), ox_ref[...].astype(BF16)], axis=1)
    x = x_ref[...] + _rms(jnp.dot(cat, wo_ref[...], preferred_element_type=F32), g1_ref[...])
    return x, _rms(x, g2_ref[...]).astype(BF16)


def _gate_up(h, wg, wu):
    g = jnp.dot(h, wg, preferred_element_type=F32)
    u = jnp.dot(h, wu, preferred_element_type=F32)
    return (g * (1.0 / (1.0 + jnp.exp(-g))) * u).astype(BF16)


def _ffn_chunk(h, wg, wu, wd):
    return jnp.dot(_gate_up(h, wg, wu), wd, preferred_element_type=F32)


def _finish_kernel(x_ref, oa_ref, ob_ref, ox_ref, wo_ref, g1_ref, g2_ref, wg_ref, wu_ref, wd_ref, g3_ref, out_ref,
                   *, ff_chunk):
    x, h = _mix_out(x_ref, oa_ref, ob_ref, ox_ref, wo_ref, g1_ref, g2_ref)
    f = 0.0
    for c0 in range(0, wg_ref.shape[1], ff_chunk):
        cols = slice(c0, c0 + ff_chunk)
        f = f + _ffn_chunk(h, wg_ref[:, cols], wu_ref[:, cols], wd_ref[cols, :])
    out_ref[...] = x + _rms(f, g3_ref[...])


def _finish(x, oa, ob, ox, wo, g1, g2, wg, wu, wd, g3, *, tm, ff_chunk):
    n, d = x.shape
    row = lambda c: pl.BlockSpec((tm, c), lambda i: (i, 0))
    const = lambda a: pl.BlockSpec(a.shape, lambda i: (0,) * a.ndim, pipeline_mode=pl.Buffered(1))
    return pl.pallas_call(
        functools.partial(_finish_kernel, ff_chunk=ff_chunk),
        grid=(n // tm,),
        in_specs=[row(d), row(Q_A), row(Q_B), row(Q_X)] + [const(a) for a in (wo, g1, g2, wg, wu, wd, g3)],
        out_specs=row(d),
        out_shape=jax.ShapeDtypeStruct((n, d), F32),
        compiler_params=_params(1),
        name="finish",
    )(x, oa, ob, ox, wo, g1, g2, wg, wu, wd, g3)


def _finish_shift_kernel(x_ref, oa_ref, ob_ref, ox_ref, wo_ref, g1_ref, g2_ref, wg_ref, wu_ref, wd_ref, g3_ref,
                         qn_ref, kn_ref, vn_ref, ck_hbm, cv_hbm, cnt_ref,
                         out_ref, obs_ref, sk_hbm, sv_hbm,
                         x1_s, h_s, f_s, act_s, kin, vin, kout, vout, sem_in, sem_out, *, s_len, ff_chunk):
    i = pl.program_id(0)
    n_tiles = pl.num_programs(0)
    r_in, r_out = kin.shape[0], kout.shape[0]
    d_ff = wg_ref.shape[1]
    n_sub = d_ff // ff_chunk + 1
    assert n_sub % r_in == 0 and n_sub % r_out == 0, "ring slots must line up across token tiles"
    base = i * n_sub

    def fetch(u, s):
        return (pltpu.make_async_copy(ck_hbm.at[u], kin.at[s], sem_in.at[0, s]),
                pltpu.make_async_copy(cv_hbm.at[u], vin.at[s], sem_in.at[1, s]))

    def flush(u, s):
        return (pltpu.make_async_copy(kout.at[s], sk_hbm.at[u], sem_out.at[0, s]),
                pltpu.make_async_copy(vout.at[s], sv_hbm.at[u], sem_out.at[1, s]))

    @pl.when(i == 0)
    def _():
        for k in range(r_in):
            for cp in fetch(k, k):
                cp.start()

    @pl.when(i > 0)
    def _():
        for cc in range(n_sub - r_out, n_sub):
            for cp in flush(base - n_sub + cc, cc % r_out):
                cp.wait()

    for cc in range(n_sub):
        u = base + cc
        s_in, s_out = cc % r_in, cc % r_out
        for cp in fetch(u, s_in):
            cp.wait()
        if cc >= r_out:
            for cp in flush(u - r_out, s_out):
                cp.wait()
        _dil_unit(qn_ref.at[u], kn_ref.at[u], vn_ref.at[u], kin.at[s_in], vin.at[s_in], cnt_ref,
                  obs_ref.at[u], kout.at[s_out], vout.at[s_out], s_len)
        prev = slice((cc - 1) * ff_chunk, cc * ff_chunk)
        cur = slice(cc * ff_chunk, (cc + 1) * ff_chunk)
        if cc == 0:
            x, h = _mix_out(x_ref, oa_ref, ob_ref, ox_ref, wo_ref, g1_ref, g2_ref)
            x1_s[...] = x
            h_s[...] = h
            act_s[...] = _gate_up(h, wg_ref[:, cur], wu_ref[:, cur])
        elif cc < n_sub - 1:
            down = jnp.dot(act_s[...], wd_ref[prev, :], preferred_element_type=F32)
            act_s[...] = _gate_up(h_s[...], wg_ref[:, cur], wu_ref[:, cur])
            f_s[...] = down if cc == 1 else f_s[...] + down
        else:
            f = f_s[...] + jnp.dot(act_s[...], wd_ref[prev, :], preferred_element_type=F32)
            out_ref[...] = x1_s[...] + _rms(f, g3_ref[...])
        for cp in flush(u, s_out):
            cp.start()
        if cc + r_in < n_sub:
            for cp in fetch(u + r_in, s_in):
                cp.start()

    @pl.when(i + 1 < n_tiles)
    def _():
        for k in range(r_in):
            for cp in fetch(base + n_sub + k, k):
                cp.start()

    @pl.when(i + 1 == n_tiles)
    def _():
        for cc in range(n_sub - r_out, n_sub):
            for cp in flush(base + cc, cc % r_out):
                cp.wait()


IN_RING = 3
OUT_RING = 2


def _finish_shift(x, oa, ob, ox, wo, g1, g2, wg, wu, wd, g3, qn, kn, vn, ck, cv, cnt, *, tm, s_len, ff_chunk):
    n, d = x.shape
    n_sub = wg.shape[1] // ff_chunk + 1
    n_units, cw, lb = ck.shape
    assert n_units == (n // tm) * n_sub, "one cache unit per sub-step"
    row = lambda c: pl.BlockSpec((tm, c), lambda i: (i, 0))
    const = lambda a: pl.BlockSpec(a.shape, lambda i: (0,) * a.ndim, pipeline_mode=pl.Buffered(1))
    whole_out = pl.BlockSpec(qn.shape, lambda i: (0, 0, 0))
    hbm = pl.BlockSpec(memory_space=pl.ANY)
    ring = lambda r: pltpu.VMEM((r, cw, lb), F32)
    return pl.pallas_call(
        functools.partial(_finish_shift_kernel, s_len=s_len, ff_chunk=ff_chunk),
        grid=(n // tm,),
        in_specs=[row(d), row(Q_A), row(Q_B), row(Q_X)] + [const(a) for a in (wo, g1, g2, wg, wu, wd, g3)]
                 + [const(a) for a in (qn, kn, vn)] + [hbm] * 2 + [const(cnt)],
        out_specs=[row(d), whole_out, hbm, hbm],
        out_shape=[jax.ShapeDtypeStruct((n, d), F32), jax.ShapeDtypeStruct(qn.shape, F32),
                   jax.ShapeDtypeStruct(ck.shape, F32), jax.ShapeDtypeStruct(cv.shape, F32)],
        scratch_shapes=[pltpu.VMEM((tm, d), F32), pltpu.VMEM((tm, d), BF16), pltpu.VMEM((tm, d), F32),
                        pltpu.VMEM((tm, ff_chunk), BF16),
                        ring(IN_RING), ring(IN_RING), ring(OUT_RING), ring(OUT_RING),
                        pltpu.SemaphoreType.DMA((2, IN_RING)), pltpu.SemaphoreType.DMA((2, OUT_RING))],
        compiler_params=_params(1),
        name="finish_shift",
    )(x, oa, ob, ox, wo, g1, g2, wg, wu, wd, g3, qn, kn, vn, ck, cv, cnt)


def _rope_tables(pos):
    half = HEAD_DIM // 2
    inv = ROPE_THETA ** (-jnp.arange(half, dtype=F32) / half)
    ang = pos.astype(F32)[:, None] * inv[None, :]
    cos = jnp.tile(jnp.cos(ang), (1, LANES // half))
    sin = jnp.tile(jnp.sin(ang), (1, LANES // half))
    first_half = (jnp.arange(LANES) % HEAD_DIM) < half
    return cos, jnp.where(first_half[None, :], -sin, sin)


def _swa_head_perm():
    return np.concatenate([np.arange(h * HEAD_DIM, (h + 1) * HEAD_DIM) for h in SWA_HEAD_ORDER])


def _seq_minor(c):
    b, n, h, dh = c.shape
    return jnp.transpose(c, (0, 2, 3, 1)).reshape(b, h * dh, n)


def _seq_major(a, heads):
    b, c, n = a.shape
    return jnp.transpose(a.reshape(b, heads, c // heads, n), (0, 3, 1, 2))


def kernel(x_prompt, x_sample, cache_swa_k, cache_swa_v, cache_dil_k, cache_dil_v, cache_mem_k, cache_mem_v,
           mem_prompt, g_pre_mix, w_in, sinks, w_mem_kv, w_o, g_post_mix, g_pre_ffn, w_gate, w_up, w_down,
           g_post_ffn):
    depth = w_in.shape[0]
    bp, t_p, d = x_prompt.shape
    bs, s_len, _ = x_sample.shape
    d_ff = w_gate.shape[2]
    perm = _swa_head_perm()
    cos_p, sin_p = _rope_tables(jnp.arange(t_p))
    cos_s, sin_s = _rope_tables(PAST_LEN + jnp.arange(s_len))
    tm_s = 256
    cos_s = jnp.tile(cos_s, (tm_s // s_len, 1))
    sin_s = jnp.tile(sin_s, (tm_s // s_len, 1))
    ff_chunk = 2 * LANES
    assert d_ff % ff_chunk == 0
    n_pairs = Q_B // LANES
    lb_a = cache_swa_k.shape[2]
    lb_b = cache_dil_k.shape[2]
    assert lb_a >= SWA_WINDOW and lb_b >= DIL_MAX_WINDOW, "every window position must lie inside the cache"
    assert lb_a % LANES == 0 and lb_b % LANES == 0 and s_len <= LANES
    amask, bcnt = (jnp.asarray(a) for a in _sample_tables(s_len, lb_a, lb_b, lb_a + LANES, lb_b + LANES))
    n_a = min(SWA_WINDOW, t_p)
    n_b = min(DIL_MAX_WINDOW, t_p)

    hp = x_prompt
    hs = x_sample.reshape(1, bs * s_len, d)
    outs = [[] for _ in range(10)]
    for l in range(depth):
        w_in_l = w_in[l]
        w_in_bf = jnp.concatenate([w_in_l[:, :Q_A][:, perm], w_in_l[:, Q_A:]], axis=1).astype(BF16)
        w_o_bf = jnp.concatenate([w_o[l][:Q_A][perm], w_o[l][Q_A:]], axis=0).astype(BF16)
        w_mem_bf = w_mem_kv[l].astype(BF16)
        wg, wu, wd = w_gate[l].astype(BF16), w_up[l].astype(BF16), w_down[l].astype(BF16)
        g0, g1, g2, g3 = (g[l][None, :] for g in (g_pre_mix, g_post_mix, g_pre_ffn, g_post_ffn))
        sink_heads = sinks[l].reshape(-1)[np.array(SWA_HEAD_ORDER)].astype(F32)
        sink_row = jnp.repeat(sink_heads, HEAD_DIM)[None, :]
        sink_rows = jnp.repeat(sink_heads, s_len)[:, None]

        qa, qb, qx, kah, vah, kbf, vbf, kat, vat, kbt, vbt = _proj(
            hp, g0, w_in_bf, cos_p, sin_p, tm=1024, prompt=True, n_a=n_a, n_b=n_b)
        qa_s, qb_s, qx_s, kan, van, kbn, vbn = (a[0] for a in _proj(
            hs, g0, w_in_bf, cos_s, sin_s, tm=tm_s, prompt=False))

        mkt, mvt, mkh, mvh = _memkv(mem_prompt, w_mem_bf)
        oa, ox = _swa_mem(qa, kah, vah, qx, mkh, mvh, sink_row, qt=1024)
        ob = _dil(qb, kbf, vbf, qt=2048)

        to_units = lambda a: jnp.transpose(a.reshape(bs, s_len, n_pairs, LANES), (0, 2, 1, 3)).reshape(
            bs * n_pairs, s_len, LANES)
        cache_units = lambda c: _seq_minor(c).reshape(bs * n_pairs, LANES, lb_b)
        flat = lambda a: a.reshape(bp * t_p, a.shape[-1])
        hp, ob_s, sbk, sbv = _finish_shift(
            flat(hp), flat(oa), flat(ob), flat(ox), w_o_bf, g1, g2, wg, wu, wd, g3,
            to_units(qb_s), to_units(kbn), to_units(vbn), cache_units(cache_dil_k[l]), cache_units(cache_dil_v[l]),
            bcnt, tm=512, s_len=s_len, ff_chunk=ff_chunk)
        hp = hp.reshape(bp, t_p, d)
        ob_s = jnp.transpose(ob_s.reshape(bs, n_pairs, s_len, LANES), (0, 2, 1, 3)).reshape(bs * s_len, Q_B)
        outs[0].append(_seq_major(kat, SWA_KV_HEADS))
        outs[1].append(_seq_major(vat, SWA_KV_HEADS))
        outs[2].append(_seq_major(kbt, DIL_HEADS))
        outs[3].append(_seq_major(vbt, DIL_HEADS))
        outs[4].append(_seq_major(mkt, MEM_HEADS))
        outs[5].append(_seq_major(mvt, MEM_HEADS))

        oa_s, ox_s, sak, sav = _sample_small(
            qa_s, kan, van, qx_s, _seq_minor(cache_swa_k[l]), _seq_minor(cache_swa_v[l]),
            _seq_minor(cache_mem_k[l]), _seq_minor(cache_mem_v[l]), sink_rows, amask, s_len=s_len, group=8)
        hs = _finish(hs[0], oa_s, ob_s, ox_s, w_o_bf, g1, g2, wg, wu, wd, g3, tm=512, ff_chunk=ff_chunk)[None]
        outs[6].append(_seq_major(sak, SWA_KV_HEADS))
        outs[7].append(_seq_major(sav, SWA_KV_HEADS))
        outs[8].append(_seq_major(sbk.reshape(bs, KV_B, lb_b), DIL_HEADS))
        outs[9].append(_seq_major(sbv.reshape(bs, KV_B, lb_b), DIL_HEADS))

    return (hp, hs.reshape(bs, s_len, d)) + tuple(jnp.stack(o) for o in outs)
```
